```python
import math
import jax, jax.numpy as jnp
from jax import lax
import numpy as np

D_MODEL = 1024
BATCH = 32
SEQ = 2048
DEPTH = 4

D_RNN = 1024
RNN_BLOCKS = 16
RNN_BLOCK = D_RNN // RNN_BLOCKS
CONV_WIDTH = 4
LRU_C = 8.0
N_HEADS = 16
QK_NOPE = 64
QK_ROPE = 32
V_HEAD = 64
Q_LORA = 384
KV_LORA = 256
ROPE_THETA = 10000.0
Q_BLOCK = 128
D_MIX = N_HEADS * V_HEAD
IN_WIDTH = 2 * D_RNN + Q_LORA + KV_LORA + QK_ROPE + 2 * D_MIX
D_FF = 3 * D_MODEL
FFN_CONV_WIDTH = 3
ALPHA = (2 * DEPTH) ** 0.25
BETA = (8 * DEPTH) ** -0.25
EPS = 1e-6
NEG_INF = -1e30

kernel_name = "hybrid_rglru_mla_convffn_deepnorm"


def _split_points():
    sizes = (D_RNN, D_RNN, Q_LORA, KV_LORA, QK_ROPE, D_MIX, D_MIX)
    pts, acc = [], 0
    for s in sizes[:-1]:
        acc += s
        pts.append(acc)
    return pts


def layer_norm(x, g, b):
    xf = x.astype(jnp.float32)
    mu = xf.mean(-1, keepdims=True)
    var = jnp.square(xf - mu).mean(-1, keepdims=True)
    y = (xf - mu) * lax.rsqrt(var + EPS) * g.astype(jnp.float32) + b.astype(jnp.float32)
    return y.astype(x.dtype)


def rms_norm(x, g):
    xf = x.astype(jnp.float32)
    y = xf * lax.rsqrt(jnp.mean(xf * xf, -1, keepdims=True) + EPS) * g.astype(jnp.float32)
    return y.astype(x.dtype)


def causal_dwconv(x, w, b):
    width, c = w.shape
    y = lax.conv_general_dilated(
        x, w[:, None, :].astype(x.dtype), window_strides=(1,), padding=[(width - 1, 0)],
        dimension_numbers=("NWC", "WIO", "NWC"), feature_group_count=c)
    return y + b.astype(x.dtype)


def rope_tables(positions, dtype):
    inv_freq = ROPE_THETA ** (-jnp.arange(0, QK_ROPE, 2, dtype=jnp.float32) / QK_ROPE)
    ang = positions.astype(jnp.float32)[..., None] * inv_freq
    return jnp.cos(ang)[:, :, None, :].astype(dtype), jnp.sin(ang)[:, :, None, :].astype(dtype)


def apply_rope(x, cos, sin):
    x1, x2 = jnp.split(x, 2, axis=-1)
    return jnp.concatenate([x1 * cos - x2 * sin, x2 * cos + x1 * sin], axis=-1)


def rg_lru(x, gx_w, gx_b, ga_w, ga_b, lru_lambda):
    B, S, _ = x.shape
    xb = x.reshape(B, S, RNN_BLOCKS, RNN_BLOCK)
    gate_x = jax.nn.sigmoid(jnp.einsum("bshi,hij->bshj", xb, gx_w).reshape(B, S, D_RNN) + gx_b)
    gate_a = jax.nn.sigmoid(jnp.einsum("bshi,hij->bshj", xb, ga_w).reshape(B, S, D_RNN) + ga_b)
    log_a = -LRU_C * gate_a.astype(jnp.float32) * jax.nn.softplus(-lru_lambda.astype(jnp.float32))
    a = jnp.exp(log_a)
    mult = jnp.sqrt(-jnp.expm1(2.0 * log_a))
    u = mult * (gate_x * x).astype(jnp.float32)

    def step(h, au):
        a_t, u_t = au
        h = a_t * h + u_t
        return h, h

    _, hs = lax.scan(step, jnp.zeros((B, D_RNN), jnp.float32),
                     (jnp.swapaxes(a, 0, 1), jnp.swapaxes(u, 0, 1)))
    return jnp.swapaxes(hs, 0, 1).astype(x.dtype)


def mla_attention(q_lat, kv_lat, k_rope_raw, cos, sin, q_norm_g, w_uq, kv_norm_g, w_ukv):
    B, S, _ = q_lat.shape
    q = (rms_norm(q_lat, q_norm_g) @ w_uq).reshape(B, S, N_HEADS, QK_NOPE + QK_ROPE)
    q_nope, q_pe = q[..., :QK_NOPE], apply_rope(q[..., QK_NOPE:], cos, sin)
    kv = (rms_norm(kv_lat, kv_norm_g) @ w_ukv).reshape(B, S, N_HEADS, QK_NOPE + V_HEAD)
    k_nope, v = kv[..., :QK_NOPE], kv[..., QK_NOPE:]
    k_pe = apply_rope(k_rope_raw[:, :, None, :], cos, sin)[:, :, 0, :]
    scale = (QK_NOPE + QK_ROPE) ** -0.5
    outs = []
    for start in range(0, S, Q_BLOCK):
        end = start + Q_BLOCK
        s = (jnp.einsum("bqhd,bkhd->bhqk", q_nope[:, start:end], k_nope[:, :end])
             + jnp.einsum("bqhd,bkd->bhqk", q_pe[:, start:end], k_pe[:, :end]))
        s = s.astype(jnp.float32) * scale
        causal = (start + jnp.arange(Q_BLOCK))[:, None] >= jnp.arange(end)[None, :]
        p = jax.nn.softmax(jnp.where(causal, s, NEG_INF), axis=-1).astype(v.dtype)
        outs.append(jnp.einsum("bhqk,bkhd->bqhd", p, v[:, :end]))
    return jnp.concatenate(outs, axis=1).reshape(B, S, D_MIX)


def mixer_sublayer(x, cos, sin, w_in, conv_w, conv_b, gx_w, gx_b, ga_w, ga_b, lru_lambda,
                   q_norm_g, w_uq, kv_norm_g, w_ukv, w_out):
    proj = x @ w_in
    x_rnn, g_rnn, q_lat, kv_lat, k_rope, gate_a, gate_b = jnp.split(proj, _split_points(), axis=-1)
    y_rnn = jax.nn.gelu(g_rnn) * rg_lru(causal_dwconv(x_rnn, conv_w, conv_b),
                                        gx_w, gx_b, ga_w, ga_b, lru_lambda)
    y_mla = mla_attention(q_lat, kv_lat, k_rope, cos, sin, q_norm_g, w_uq, kv_norm_g, w_ukv)
    merged = jax.nn.sigmoid(gate_a) * y_rnn + jax.nn.sigmoid(gate_b) * y_mla
    return merged @ w_out


def conv_ffn(x, w_up, ffn_conv_w, ffn_conv_b, w_down):
    h = causal_dwconv(x @ w_up, ffn_conv_w, ffn_conv_b)
    h_gate, h_val = jnp.split(h, 2, axis=-1)
    return (jax.nn.gelu(h_gate) * h_val) @ w_down


def setup_inputs(seed: int = 0) -> dict:
    key = jax.random.key(seed)
    ks = jax.random.split(key, 24)
    f32 = jnp.float32

    def nrm(k, shape, scale):
        return jax.random.normal(k, shape, f32) * scale

    x = jax.random.normal(ks[0], (BATCH, SEQ, D_MODEL), f32)
    offsets = jax.random.randint(ks[1], (BATCH, 1), 0, 4096, dtype=jnp.int32)
    positions = (jnp.arange(SEQ, dtype=jnp.int32)[None, :] + offsets).astype(jnp.int32)
    u = jax.random.uniform(ks[2], (DEPTH, D_RNN), f32, 0.9, 0.999)
    a0 = u ** (1.0 / LRU_C)
    lru_lambda = jnp.log(a0) - jnp.log1p(-a0)
    return {
        "x": x,
        "positions": positions,
        "w_in": nrm(ks[3], (DEPTH, D_MODEL, IN_WIDTH), D_MODEL ** -0.5),
        "conv_w": nrm(ks[4], (DEPTH, CONV_WIDTH, D_RNN), CONV_WIDTH ** -0.5),
        "conv_b": nrm(ks[5], (DEPTH, D_RNN), 0.02),
        "gx_w": nrm(ks[6], (DEPTH, RNN_BLOCKS, RNN_BLOCK, RNN_BLOCK), RNN_BLOCK ** -0.5),
        "gx_b": nrm(ks[7], (DEPTH, D_RNN), 0.02),
        "ga_w": nrm(ks[8], (DEPTH, RNN_BLOCKS, RNN_BLOCK, RNN_BLOCK), RNN_BLOCK ** -0.5),
        "ga_b": nrm(ks[9], (DEPTH, D_RNN), 0.02),
        "lru_lambda": lru_lambda,
        "q_norm_g": 1.0 + nrm(ks[10], (DEPTH, Q_LORA), 0.02),
        "w_uq": nrm(ks[11], (DEPTH, Q_LORA, N_HEADS * (QK_NOPE + QK_ROPE)), Q_LORA ** -0.5),
        "kv_norm_g": 1.0 + nrm(ks[12], (DEPTH, KV_LORA), 0.02),
        "w_ukv": nrm(ks[13], (DEPTH, KV_LORA, N_HEADS * (QK_NOPE + V_HEAD)), KV_LORA ** -0.5),
        "w_out": nrm(ks[14], (DEPTH, D_MIX, D_MODEL), BETA * D_MIX ** -0.5),
        "ln1_g": 1.0 + nrm(ks[15], (DEPTH, D_MODEL), 0.02),
        "ln1_b": nrm(ks[16], (DEPTH, D_MODEL), 0.02),
        "w_up": nrm(ks[17], (DEPTH, D_MODEL, 2 * D_FF), D_MODEL ** -0.5),
        "ffn_conv_w": nrm(ks[18], (DEPTH, FFN_CONV_WIDTH, 2 * D_FF), FFN_CONV_WIDTH ** -0.5),
        "ffn_conv_b": nrm(ks[19], (DEPTH, 2 * D_FF), 0.02),
        "w_down": nrm(ks[20], (DEPTH, D_FF, D_MODEL), BETA * D_FF ** -0.5),
        "ln2_g": 1.0 + nrm(ks[21], (DEPTH, D_MODEL), 0.02),
        "ln2_b": nrm(ks[22], (DEPTH, D_MODEL), 0.02),
    }


def reference(x, positions, w_in, conv_w, conv_b, gx_w, gx_b, ga_w, ga_b, lru_lambda,
              q_norm_g, w_uq, kv_norm_g, w_ukv, w_out, ln1_g, ln1_b,
              w_up, ffn_conv_w, ffn_conv_b, w_down, ln2_g, ln2_b):
    cos, sin = rope_tables(positions, x.dtype)
    for l in range(DEPTH):
        mix = mixer_sublayer(x, cos, sin, w_in[l], conv_w[l], conv_b[l], gx_w[l], gx_b[l],
                             ga_w[l], ga_b[l], lru_lambda[l], q_norm_g[l], w_uq[l],
                             kv_norm_g[l], w_ukv[l], w_out[l])
        x = layer_norm(ALPHA * x + mix, ln1_g[l], ln1_b[l])
        ffn = conv_ffn(x, w_up[l], ffn_conv_w[l], ffn_conv_b[l], w_down[l])
        x = layer_norm(ALPHA * x + ffn, ln2_g[l], ln2_b[l])
    return x
```

```python
import functools
import math

import jax
import jax.numpy as jnp
import numpy as np
from jax import lax
from jax.experimental import pallas as pl
from jax.experimental.pallas import tpu as pltpu

RNN_BLOCKS = 16
CONV_WIDTH = 4
LRU_C = 8.0
N_HEADS = 16
QK_NOPE = 64
QK_ROPE = 32
V_HEAD = 64
ROPE_THETA = 10000.0
FFN_CONV_WIDTH = 3
EPS = 1e-6
NEG_INF = -1e30

LANES = 128
SUBLANES = 8
HEAD_GROUP = LANES
ROPE_LO = QK_NOPE
ROPE_MID = QK_NOPE + QK_ROPE // 2
ROPE_HI = QK_NOPE + QK_ROPE
VMEM_LIMIT = 56 * 1024 * 1024

BF16 = jnp.bfloat16
F32 = jnp.float32


def _dot(a, b):
    return jnp.dot(a, b, preferred_element_type=F32)


def _gelu_tanh(x):
    c = math.sqrt(2.0 / math.pi)
    return x * (0.5 * (1.0 + jnp.tanh(c * (x + 0.044715 * (x * x * x)))))


def _sigmoid(x):
    return 1.0 / (1.0 + jnp.exp(-x))


def _layer_norm(z, g, b):
    mu = jnp.mean(z, axis=-1, keepdims=True)
    zc = z - mu
    var = jnp.mean(zc * zc, axis=-1, keepdims=True)
    return zc * lax.rsqrt(var + EPS) * g + b


def _rms_norm(z, g):
    return z * lax.rsqrt(jnp.mean(z * z, axis=-1, keepdims=True) + EPS) * g


def _shifted(prev_rows, cur, shift):
    ext = jnp.concatenate([prev_rows, cur], axis=0)
    return pltpu.roll(ext, shift, 0)[SUBLANES:]


def _const_spec(shape):
    nd = len(shape)
    return pl.BlockSpec(shape, lambda *_: (0,) * nd, pipeline_mode=pl.Buffered(1))


def _tok_spec(tm, width):
    return pl.BlockSpec((None, tm, width), lambda b, s: (b, s, 0))


def _rope_tables_body(pos_ref, invf_ref, cos_ref, sin_ref):
    pos = pos_ref[...].astype(F32)
    ang = pos * invf_ref[...]
    lane = lax.broadcasted_iota(jnp.int32, ang.shape, 1)
    cosv = jnp.cos(ang)
    sinv = jnp.sin(ang)
    rot = (lane >= ROPE_LO) & (lane < ROPE_HI)
    cos_ref[...] = jnp.where(rot, cosv, 1.0)
    sin_ref[...] = jnp.where(rot, jnp.where(lane < ROPE_MID, -sinv, sinv), 0.0)


def _rope_tables(positions, tm):
    B, S = positions.shape
    half = QK_ROPE // 2
    inv_freq = ROPE_THETA ** (-jnp.arange(0, QK_ROPE, 2, dtype=F32) / QK_ROPE)
    lane = np.arange(LANES)
    idx = np.clip(lane - ROPE_LO, 0, QK_ROPE - 1) % half
    invf = inv_freq[idx].reshape(1, LANES)
    pos = positions.reshape(B, S, 1)
    out = jax.ShapeDtypeStruct((B, S, LANES), F32)
    return pl.pallas_call(
        _rope_tables_body,
        grid=(B, S // tm),
        in_specs=[_tok_spec(tm, 1), pl.BlockSpec((1, LANES), lambda b, s: (0, 0))],
        out_specs=[_tok_spec(tm, LANES), _tok_spec(tm, LANES)],
        out_shape=[out, out],
        compiler_params=pltpu.CompilerParams(dimension_semantics=("parallel", "parallel")),
        name="rope_tables",
    )(pos, invf)


def _lru_scan(a, u, h0):
    tm, C = a.shape
    groups = tm // SUBLANES
    a3 = a.reshape(groups, SUBLANES, C)
    u3 = u.reshape(groups, SUBLANES, C)
    sub = lax.broadcasted_iota(jnp.int32, a3.shape, 1)
    step = 1
    while step < SUBLANES:
        m = sub >= step
        ra = pltpu.roll(a3, step, 1)
        ru = pltpu.roll(u3, step, 1)
        u3 = jnp.where(m, a3 * ru + u3, u3)
        a3 = jnp.where(m, a3 * ra, a3)
        step *= 2
    hprev = h0
    outs = []
    for g in range(groups):
        hg = a3[g] * hprev + u3[g]
        outs.append(hg)
        hprev = hg[SUBLANES - 1:SUBLANES]
    return jnp.concatenate(outs, axis=0), hprev


def _in_proj_body(d_rnn, d_mix, q_lora, kv_lora, chunk,
                  x_ref, w_ref, cw_ref, cb_ref, wg_ref, gxb_ref, gab_ref, lam_ref, qg_ref, kvg_ref,
                  pa_ref, gb_ref, qn_ref, kvn_ref, kpe_ref, xprev_ref, hc_ref):
    @pl.when(pl.program_id(1) == 0)
    def _():
        xprev_ref[...] = jnp.zeros_like(xprev_ref)
        hc_ref[...] = jnp.zeros_like(hc_ref)

    tm = x_ref.shape[0]
    xb = x_ref[...].astype(BF16)
    neg_lam = -lam_ref[...]
    softplus = jnp.maximum(neg_lam, 0.0) + jnp.log1p(jnp.exp(-jnp.abs(neg_lam)))
    off_g, off_a, off_b = d_rnn, 2 * d_rnn, 2 * d_rnn + d_mix
    off_q = 2 * d_rnn + 2 * d_mix
    off_kv = off_q + q_lora
    off_kpe = off_kv + kv_lora

    for c in range(d_rnn // chunk):
        cs = slice(c * chunk, (c + 1) * chunk)
        xr = _dot(xb, w_ref[:, cs])
        prev = xprev_ref[:, cs]
        conv = xr * cw_ref[CONV_WIDTH - 1:CONV_WIDTH, cs] + cb_ref[:, cs]
        for k in range(1, CONV_WIDTH):
            conv = conv + _shifted(prev, xr, k) * cw_ref[CONV_WIDTH - 1 - k:CONV_WIDTH - k, cs]
        xprev_ref[:, cs] = xr[tm - SUBLANES:]

        gates = _dot(conv.astype(BF16), wg_ref[c])
        gate_x = _sigmoid(gates[:, :chunk] + gxb_ref[:, cs])
        gate_r = _sigmoid(gates[:, chunk:] + gab_ref[:, cs])
        log_a = (-LRU_C) * gate_r * softplus[:, cs]
        a = jnp.exp(log_a)
        t = jnp.tanh(log_a)
        mult = jnp.sqrt((-2.0 * t) / (1.0 - t))
        u = mult * (gate_x * conv)
        h, hlast = _lru_scan(a, u, hc_ref[:, cs])
        hc_ref[:, cs] = hlast

        g_rnn = _dot(xb, w_ref[:, off_g + c * chunk:off_g + (c + 1) * chunk])
        y_rnn = _gelu_tanh(g_rnn) * h
        gate_a = _dot(xb, w_ref[:, off_a + c * chunk:off_a + (c + 1) * chunk])
        pa_ref[:, cs] = _sigmoid(gate_a) * y_rnn
        gb_ref[:, cs] = _dot(xb, w_ref[:, off_b + c * chunk:off_b + (c + 1) * chunk])

    q_lat = _dot(xb, w_ref[:, off_q:off_kv])
    qn_ref[...] = _rms_norm(q_lat, qg_ref[...]).astype(BF16)
    kv_lat = _dot(xb, w_ref[:, off_kv:off_kpe])
    kvn_ref[...] = _rms_norm(kv_lat, kvg_ref[...]).astype(BF16)
    kpe_ref[...] = _dot(xb, w_ref[:, off_kpe:off_kpe + HEAD_GROUP])


def _in_proj_rglru(x, w_in_p, conv_w, conv_b, wg, gx_b, ga_b, lam, qg, kvg, *, tm, chunk):
    B, S, D = x.shape
    d_rnn = conv_w.shape[1]
    q_lora, kv_lora = qg.shape[1], kvg.shape[1]
    d_mix = (w_in_p.shape[1] - 2 * d_rnn - q_lora - kv_lora - HEAD_GROUP) // 2
    body = functools.partial(_in_proj_body, d_rnn, d_mix, q_lora, kv_lora, chunk)
    ins = [x, w_in_p, conv_w, conv_b, wg, gx_b, ga_b, lam, qg, kvg]
    in_specs = [_tok_spec(tm, D)] + [_const_spec(a.shape) for a in ins[1:]]
    outs = [(d_mix, F32), (d_mix, F32), (q_lora, BF16), (kv_lora, BF16), (HEAD_GROUP, F32)]
    return pl.pallas_call(
        body,
        grid=(B, S // tm),
        in_specs=in_specs,
        out_specs=[_tok_spec(tm, w) for w, _ in outs],
        out_shape=[jax.ShapeDtypeStruct((B, S, w), dt) for w, dt in outs],
        scratch_shapes=[pltpu.VMEM((SUBLANES, d_rnn), F32), pltpu.VMEM((1, d_rnn), F32)],
        compiler_params=pltpu.CompilerParams(
            dimension_semantics=("arbitrary", "arbitrary"), vmem_limit_bytes=VMEM_LIMIT),
        name="in_proj_rglru",
    )(*ins)


def _qkv_body(qn_ref, kvn_ref, kpe_ref, cos_ref, sin_ref, wq_ref, wk_ref, wv_ref, q_ref, k_ref, v_ref):
    cosv = cos_ref[...]
    sinv = sin_ref[...]
    lane = lax.broadcasted_iota(jnp.int32, cosv.shape, 1)
    first_half = lane < ROPE_MID

    def rope(z):
        partner = jnp.where(first_half,
                            pltpu.roll(z, HEAD_GROUP - QK_ROPE // 2, 1),
                            pltpu.roll(z, QK_ROPE // 2, 1))
        return z * cosv + partner * sinv

    kpe = rope(kpe_ref[...])
    kvn = kvn_ref[...]
    qf = _dot(qn_ref[...], wq_ref[...])
    kf = _dot(kvn, wk_ref[...])
    for h in range(qf.shape[1] // HEAD_GROUP):
        sl = slice(h * HEAD_GROUP, (h + 1) * HEAD_GROUP)
        q_ref[:, sl] = rope(qf[:, sl]).astype(BF16)
        k_ref[:, sl] = (kf[:, sl] + kpe).astype(BF16)
    v_ref[...] = _dot(kvn, wv_ref[...]).astype(BF16)


def _qkv_up(qn, kvn, kpe, cos_t, sin_t, wq, wk, wv, *, tm):
    B, S, _ = qn.shape
    ins = [qn, kvn, kpe, cos_t, sin_t, wq, wk, wv]
    in_specs = [_tok_spec(tm, a.shape[2]) for a in ins[:5]] + [_const_spec(a.shape) for a in ins[5:]]
    widths = [wq.shape[1], wk.shape[1], wv.shape[1]]
    return pl.pallas_call(
        _qkv_body,
        grid=(B, S // tm),
        in_specs=in_specs,
        out_specs=[_tok_spec(tm, w) for w in widths],
        out_shape=[jax.ShapeDtypeStruct((B, S, w), BF16) for w in widths],
        compiler_params=pltpu.CompilerParams(
            dimension_semantics=("parallel", "parallel"), vmem_limit_bytes=VMEM_LIMIT),
        name="qkv_up",
    )(*ins)


def _attn_body(alpha, q_ref, k_ref, v_ref, pa_ref, gb_ref, x_ref, wo_ref, g_ref, b_ref, o_ref):
    tq = q_ref.shape[0]
    tk = tq
    qi = pl.program_id(1)
    scale = (QK_NOPE + QK_ROPE) ** -0.5
    row = lax.broadcasted_iota(jnp.int32, (tq, tk), 0)
    col = lax.broadcasted_iota(jnp.int32, (tq, tk), 1)
    causal = row >= col
    lane = lax.broadcasted_iota(jnp.int32, (tq, HEAD_GROUP), 1)
    n_pairs = v_ref.shape[1] // HEAD_GROUP

    def head(h, p):
        ksl = slice(h * HEAD_GROUP, (h + 1) * HEAD_GROUP)
        vsl = slice(p * HEAD_GROUP, (p + 1) * HEAD_GROUP)
        qh = q_ref[:, ksl]

        def block(j, carry, masked):
            m, l, acc = carry
            off = pl.multiple_of(j * tk, tk)
            kj = k_ref[pl.ds(off, tk), ksl]
            vj = v_ref[pl.ds(off, tk), vsl]
            s = lax.dot_general(qh, kj, (((1,), (1,)), ((), ())), preferred_element_type=F32) * scale
            if masked:
                s = jnp.where(causal, s, NEG_INF)
            mn = jnp.maximum(m, jnp.max(s, axis=-1, keepdims=True))
            corr = jnp.exp(m - mn)
            pexp = jnp.exp(s - mn)
            l = corr * l + jnp.sum(pexp, axis=-1, keepdims=True)
            acc = corr * acc + _dot(pexp.astype(BF16), vj)
            return mn, l, acc

        init = (jnp.full((tq, 1), NEG_INF, F32), jnp.zeros((tq, 1), F32), jnp.zeros((tq, HEAD_GROUP), F32))
        carry = lax.fori_loop(0, qi, lambda j, c: block(j, c, False), init)
        _, l, acc = block(qi, carry, True)
        return acc / l

    ys = []
    for p in range(n_pairs):
        ya = head(2 * p, p)
        yb = head(2 * p + 1, p)
        ys.append(jnp.where(lane < V_HEAD, ya, yb))
    y = jnp.concatenate(ys, axis=1)
    merged = pa_ref[...] + _sigmoid(gb_ref[...]) * y
    o = _dot(merged.astype(BF16), wo_ref[...])
    o_ref[...] = _layer_norm(alpha * x_ref[...] + o, g_ref[...], b_ref[...])


def _attn_out(q, k, v, pa, gb, x, w_out, g, b, *, tq, alpha):
    B, S, D = x.shape
    full = lambda a: pl.BlockSpec((None, S, a.shape[2]), lambda bi, s: (bi, 0, 0))
    in_specs = [_tok_spec(tq, q.shape[2]), full(k), full(v), _tok_spec(tq, pa.shape[2]),
                _tok_spec(tq, gb.shape[2]), _tok_spec(tq, D),
                _const_spec(w_out.shape), _const_spec(g.shape), _const_spec(b.shape)]
    return pl.pallas_call(
        functools.partial(_attn_body, alpha),
        grid=(B, S // tq),
        in_specs=in_specs,
        out_specs=_tok_spec(tq, D),
        out_shape=jax.ShapeDtypeStruct((B, S, D), F32),
        compiler_params=pltpu.CompilerParams(
            dimension_semantics=("parallel", "parallel"), vmem_limit_bytes=VMEM_LIMIT),
        name="attn_out",
    )(q, k, v, pa, gb, x, w_out, g, b)


def _ffn_body(alpha, chunk, x_ref, wu_ref, cw_ref, cb_ref, wd_ref, g_ref, b_ref, o_ref, hprev_ref):
    @pl.when(pl.program_id(1) == 0)
    def _():
        hprev_ref[...] = jnp.zeros_like(hprev_ref)

    tm = x_ref.shape[0]
    d_ff = wd_ref.shape[0]
    x = x_ref[...]
    xb = x.astype(BF16)

    def up_conv(lo):
        cs = slice(lo, lo + chunk)
        hcur = _dot(xb, wu_ref[:, cs])
        prev = hprev_ref[:, cs]
        out = hcur * cw_ref[FFN_CONV_WIDTH - 1:FFN_CONV_WIDTH, cs] + cb_ref[:, cs]
        for k in range(1, FFN_CONV_WIDTH):
            out = out + _shifted(prev, hcur, k) * cw_ref[FFN_CONV_WIDTH - 1 - k:FFN_CONV_WIDTH - k, cs]
        hprev_ref[:, cs] = hcur[tm - SUBLANES:]
        return out

    acc = jnp.zeros(o_ref.shape, F32)
    for c in range(d_ff // chunk):
        h_gate = up_conv(c * chunk)
        h_val = up_conv(d_ff + c * chunk)
        act = (_gelu_tanh(h_gate) * h_val).astype(BF16)
        acc = acc + _dot(act, wd_ref[c * chunk:(c + 1) * chunk, :])
    o_ref[...] = _layer_norm(alpha * x + acc, g_ref[...], b_ref[...])


def _conv_ffn(x, w_up, cw, cb, w_down, g, b, *, tm, chunk, alpha):
    B, S, D = x.shape
    ins = [x, w_up, cw, cb, w_down, g, b]
    return pl.pallas_call(
        functools.partial(_ffn_body, alpha, chunk),
        grid=(B, S // tm),
        in_specs=[_tok_spec(tm, D)] + [_const_spec(a.shape) for a in ins[1:]],
        out_specs=_tok_spec(tm, D),
        out_shape=jax.ShapeDtypeStruct((B, S, D), F32),
        scratch_shapes=[pltpu.VMEM((SUBLANES, w_up.shape[1]), F32)],
        compiler_params=pltpu.CompilerParams(
            dimension_semantics=("arbitrary", "arbitrary"), vmem_limit_bytes=VMEM_LIMIT),
        name="conv_ffn",
    )(*ins)


def _block_diag(w):
    H, r, _ = w.shape
    eye = jnp.eye(H, dtype=w.dtype)
    return (eye[:, None, :, None] * w[:, :, None, :]).reshape(H * r, H * r)


def _layer_weights(l, chunk, w_in, conv_w, conv_b, gx_w, gx_b, ga_w, ga_b, lru_lambda,
                   q_norm_g, w_uq, kv_norm_g, w_ukv, w_out, ln1_g, ln1_b,
                   w_up, ffn_conv_w, ffn_conv_b, w_down, ln2_g, ln2_b):
    d_model = w_in.shape[1]
    d_rnn = conv_w.shape[2]
    q_lora, kv_lora = w_uq.shape[1], w_ukv.shape[1]
    d_mix = w_out.shape[1]
    p_q = 2 * d_rnn
    p_kv = p_q + q_lora
    p_kr = p_kv + kv_lora
    p_a = p_kr + QK_ROPE
    w = w_in[l]
    zeros = lambda n: jnp.zeros((d_model, n), w.dtype)
    w_in_p = jnp.concatenate(
        [w[:, :p_q], w[:, p_a:], w[:, p_q:p_kr], zeros(ROPE_LO), w[:, p_kr:p_a], zeros(HEAD_GROUP - ROPE_HI)],
        axis=1).astype(BF16)
    gx_bd, ga_bd = _block_diag(gx_w[l]), _block_diag(ga_w[l])
    wg = jnp.stack([
        jnp.concatenate([gx_bd[c * chunk:(c + 1) * chunk, c * chunk:(c + 1) * chunk],
                         ga_bd[c * chunk:(c + 1) * chunk, c * chunk:(c + 1) * chunk]], axis=1)
        for c in range(d_rnn // chunk)]).astype(BF16)
    row = lambda a: a[l].reshape(1, -1)
    wq = w_uq[l].reshape(q_lora, N_HEADS, QK_NOPE + QK_ROPE)
    wq = jnp.pad(wq, ((0, 0), (0, 0), (0, HEAD_GROUP - ROPE_HI))).reshape(q_lora, N_HEADS * HEAD_GROUP).astype(BF16)
    wkv = w_ukv[l].reshape(kv_lora, N_HEADS, QK_NOPE + V_HEAD)
    wk = jnp.pad(wkv[:, :, :QK_NOPE], ((0, 0), (0, 0), (0, HEAD_GROUP - QK_NOPE)))
    wk = wk.reshape(kv_lora, N_HEADS * HEAD_GROUP).astype(BF16)
    wv = wkv[:, :, QK_NOPE:].reshape(kv_lora, N_HEADS * V_HEAD).astype(BF16)
    return dict(
        w_in_p=w_in_p, conv_w=conv_w[l], conv_b=row(conv_b), wg=wg, gx_b=row(gx_b), ga_b=row(ga_b),
        lam=row(lru_lambda), qg=row(q_norm_g), kvg=row(kv_norm_g), wq=wq, wk=wk, wv=wv,
        w_out=w_out[l].astype(BF16), ln1_g=row(ln1_g), ln1_b=row(ln1_b),
        w_up=w_up[l].astype(BF16), fcw=ffn_conv_w[l], fcb=row(ffn_conv_b), w_down=w_down[l].astype(BF16),
        ln2_g=row(ln2_g), ln2_b=row(ln2_b))


def kernel(x, positions, w_in, conv_w, conv_b, gx_w, gx_b, ga_w, ga_b, lru_lambda, q_norm_g, w_uq, kv_norm_g, w_ukv, w_out, ln1_g, ln1_b, w_up, ffn_conv_w, ffn_conv_b, w_down, ln2_g, ln2_b):
    depth = w_in.shape[0]
    S = x.shape[1]
    alpha = (2 * depth) ** 0.25
    tm = min(256, S)
    tq = min(256, S)
    t_up = min(512, S)
    rnn_chunk = 256
    ffn_chunk = 512
    cos_t, sin_t = _rope_tables(positions, t_up)
    for l in range(depth):
        p = _layer_weights(l, rnn_chunk, w_in, conv_w, conv_b, gx_w, gx_b, ga_w, ga_b, lru_lambda,
                           q_norm_g, w_uq, kv_norm_g, w_ukv, w_out, ln1_g, ln1_b,
                           w_up, ffn_conv_w, ffn_conv_b, w_down, ln2_g, ln2_b)
        pa, gb, qn, kvn, kpe = _in_proj_rglru(
            x, p["w_in_p"], p["conv_w"], p["conv_b"], p["wg"], p["gx_b"], p["ga_b"], p["lam"],
            p["qg"], p["kvg"], tm=tm, chunk=rnn_chunk)
        q, k, v = _qkv_up(qn, kvn, kpe, cos_t, sin_t, p["wq"], p["wk"], p["wv"], tm=t_up)
        x = _attn_out(q, k, v, pa, gb, x, p["w_out"], p["ln1_g"], p["ln1_b"], tq=tq, alpha=alpha)
        x = _conv_ffn(x, p["w_up"], p["fcw"], p["fcb"], p["w_down"], p["ln2_g"], p["ln2_b"],
                      tm=tm, chunk=ffn_chunk, alpha=alpha)
    return x
```

```python
import functools
import math

import jax
import jax.numpy as jnp
import numpy as np
from jax import lax
from jax.experimental import pallas as pl
from jax.experimental.pallas import tpu as pltpu

RNN_BLOCKS = 16
CONV_WIDTH = 4
LRU_C = 8.0
N_HEADS = 16
QK_NOPE = 64
QK_ROPE = 32
V_HEAD = 64
ROPE_THETA = 10000.0
FFN_CONV_WIDTH = 3
EPS = 1e-6
NEG_INF = -1e30

LANES = 128
SUBLANES = 8
TB = 256
ROWS = TB // SUBLANES
HEAD_GROUP = LANES
ROPE_HALF = QK_ROPE // 2
HALF_GROUP = HEAD_GROUP // 2
V_ROWS = 80
VMEM_LIMIT = 56 * 1024 * 1024
SCORE_SCALE = (QK_NOPE + QK_ROPE) ** -0.5 * math.log2(math.e)

BF16 = jnp.bfloat16
F32 = jnp.float32


def _dot(a, b):
    return jnp.dot(a, b, preferred_element_type=F32)


def _gelu_tanh(x):
    c = math.sqrt(2.0 / math.pi)
    return x * (0.5 * (1.0 + jnp.tanh(c * (x + 0.044715 * (x * x * x)))))


def _sigmoid(x):
    return 1.0 / (1.0 + jnp.exp(-x))


def _layer_norm(z, g, b):
    mu = jnp.mean(z, axis=-1, keepdims=True)
    zc = z - mu
    var = jnp.mean(zc * zc, axis=-1, keepdims=True)
    return zc * lax.rsqrt(var + EPS) * g + b


def _rms_norm(z, g):
    return z * lax.rsqrt(jnp.mean(z * z, axis=-1, keepdims=True) + EPS) * g


def _to_strands(a):
    B, S = a.shape[:2]
    rest = a.shape[2:]
    return a.reshape(B, S // TB, SUBLANES, ROWS, *rest).swapaxes(2, 3).reshape(B, S, *rest)


def _from_strands(a):
    B, S = a.shape[:2]
    rest = a.shape[2:]
    return a.reshape(B, S // TB, ROWS, SUBLANES, *rest).swapaxes(2, 3).reshape(B, S, *rest)


def _strand_time(r):
    return (r & (SUBLANES - 1)) * ROWS + (r >> 3)


def _delays(prev_tail, x, depth):
    n = depth * SUBLANES
    C = x.shape[1]
    sub = lax.broadcasted_iota(jnp.int32, (depth, SUBLANES, C), 1)
    own = pltpu.roll(x[TB - n:].reshape(depth, SUBLANES, C), 1, 1)
    prev = pltpu.roll(prev_tail.reshape(depth, SUBLANES, C), 1, 1)
    head = jnp.where(sub == 0, prev, own).reshape(n, C)
    ext = jnp.concatenate([head, x], axis=0)
    return [ext[(depth - k) * SUBLANES:(depth - k) * SUBLANES + TB] for k in range(1, depth + 1)]


def _causal_conv(prev_tail, x, w_ref, b_ref, cs, width):
    out = x * w_ref[width - 1:width, cs] + b_ref[:, cs]
    for k, xd in enumerate(_delays(prev_tail, x, width - 1), start=1):
        out = out + xd * w_ref[width - 1 - k:width - k, cs]
    return out


def _const_spec(shape):
    nd = len(shape)
    return pl.BlockSpec(shape, lambda *_: (0,) * nd, pipeline_mode=pl.Buffered(1))


def _tok_spec(tm, width):
    return pl.BlockSpec((None, tm, width), lambda b, s: (b, s, 0))


def _rope_tables_body(pos_ref, invf_ref, cos_ref, sin_ref):
    pos = pos_ref[...].astype(F32)
    ang = pos * invf_ref[...]
    lane = lax.broadcasted_iota(jnp.int32, ang.shape, 1)
    cosv = jnp.cos(ang)
    sinv = jnp.sin(ang)
    rot = (lane & (HALF_GROUP - 1)) < ROPE_HALF
    cos_ref[...] = jnp.where(rot, cosv, 1.0)
    sin_ref[...] = jnp.where(rot, jnp.where(lane < HALF_GROUP, -sinv, sinv), 0.0)


def _rope_tables(positions, tm):
    B, S = positions.shape
    inv_freq = ROPE_THETA ** (-jnp.arange(0, QK_ROPE, 2, dtype=F32) / QK_ROPE)
    idx = np.minimum(np.arange(LANES) % HALF_GROUP, ROPE_HALF - 1)
    invf = inv_freq[idx].reshape(1, LANES)
    pos = positions.reshape(B, S, 1)
    out = jax.ShapeDtypeStruct((B, S, LANES), F32)
    return pl.pallas_call(
        _rope_tables_body,
        grid=(B, S // tm),
        in_specs=[_tok_spec(tm, 1), pl.BlockSpec((1, LANES), lambda b, s: (0, 0))],
        out_specs=[_tok_spec(tm, LANES), _tok_spec(tm, LANES)],
        out_shape=[out, out],
        compiler_params=pltpu.CompilerParams(dimension_semantics=("parallel", "parallel")),
        name="rope_tables",
    )(pos, invf)


def _lru_scan(a, u, h0):
    C = a.shape[1]
    row = lambda z, v: z[v * SUBLANES:(v + 1) * SUBLANES]
    p, h = row(a, 0), row(u, 0)
    ps, hs = [p], [h]
    for v in range(1, ROWS):
        av = row(a, v)
        p = av * p
        h = av * h + row(u, v)
        ps.append(p)
        hs.append(h)
    sub = lax.broadcasted_iota(jnp.int32, (SUBLANES, C), 0)
    pc, hc = p, h
    step = 1
    while step < SUBLANES:
        m = sub >= step
        rp = pltpu.roll(pc, step, 0)
        rh = pltpu.roll(hc, step, 0)
        hc = jnp.where(m, pc * rh + hc, hc)
        pc = jnp.where(m, pc * rp, pc)
        step *= 2
    end = hc + pc * h0
    init = jnp.where(sub == 0, h0, pltpu.roll(end, 1, 0))
    out = jnp.concatenate([hs[v] + ps[v] * init for v in range(ROWS)], axis=0)
    return out, end[SUBLANES - 1:]


def _in_proj_body(d_rnn, d_mix, q_lora, kv_lora, chunk,
                  x_ref, w_ref, cw_ref, cb_ref, wg_ref, gxb_ref, gab_ref, lam_ref, qg_ref, kvg_ref,
                  pa_ref, gb_ref, qn_ref, kvn_ref, kpe_ref, xprev_ref, hc_ref):
    @pl.when(pl.program_id(1) == 0)
    def _():
        xprev_ref[...] = jnp.zeros_like(xprev_ref)
        hc_ref[...] = jnp.zeros_like(hc_ref)

    tail = xprev_ref.shape[0]
    xb = x_ref[...].astype(BF16)
    neg_lam = -lam_ref[...]
    softplus = jnp.maximum(neg_lam, 0.0) + jnp.log1p(jnp.exp(-jnp.abs(neg_lam)))
    off_g, off_a, off_b = d_rnn, 2 * d_rnn, 2 * d_rnn + d_mix
    off_q = 2 * d_rnn + 2 * d_mix
    off_kv = off_q + q_lora
    off_kpe = off_kv + kv_lora

    for c in range(d_rnn // chunk):
        cs = slice(c * chunk, (c + 1) * chunk)
        xr = _dot(xb, w_ref[:, cs])
        conv = _causal_conv(xprev_ref[:, cs], xr, cw_ref, cb_ref, cs, CONV_WIDTH)
        xprev_ref[:, cs] = xr[TB - tail:]

        gates = _dot(conv.astype(BF16), wg_ref[c])
        gate_x = _sigmoid(gates[:, :chunk] + gxb_ref[:, cs])
        gate_r = _sigmoid(gates[:, chunk:] + gab_ref[:, cs])
        log_a = (-LRU_C) * gate_r * softplus[:, cs]
        a = jnp.exp(log_a)
        t = jnp.tanh(log_a)
        mult = jnp.sqrt((-2.0 * t) / (1.0 - t))
        u = mult * (gate_x * conv)
        h, hlast = _lru_scan(a, u, hc_ref[:, cs])
        hc_ref[:, cs] = hlast

        g_rnn = _dot(xb, w_ref[:, off_g + c * chunk:off_g + (c + 1) * chunk])
        y_rnn = _gelu_tanh(g_rnn) * h
        gate_a = _dot(xb, w_ref[:, off_a + c * chunk:off_a + (c + 1) * chunk])
        pa_ref[:, cs] = _sigmoid(gate_a) * y_rnn
        gb_ref[:, cs] = _dot(xb, w_ref[:, off_b + c * chunk:off_b + (c + 1) * chunk])

    q_lat = _dot(xb, w_ref[:, off_q:off_kv])
    qn_ref[...] = _rms_norm(q_lat, qg_ref[...]).astype(BF16)
    kv_lat = _dot(xb, w_ref[:, off_kv:off_kpe])
    kvn_ref[...] = _rms_norm(kv_lat, kvg_ref[...]).astype(BF16)
    kpe_ref[...] = _dot(xb, w_ref[:, off_kpe:off_kpe + HEAD_GROUP])


def _in_proj_rglru(x, w_in_p, conv_w, conv_b, wg, gx_b, ga_b, lam, qg, kvg, *, chunk):
    B, S, D = x.shape
    d_rnn = conv_w.shape[1]
    q_lora, kv_lora = qg.shape[1], kvg.shape[1]
    d_mix = (w_in_p.shape[1] - 2 * d_rnn - q_lora - kv_lora - HEAD_GROUP) // 2
    body = functools.partial(_in_proj_body, d_rnn, d_mix, q_lora, kv_lora, chunk)
    ins = [x, w_in_p, conv_w, conv_b, wg, gx_b, ga_b, lam, qg, kvg]
    in_specs = [_tok_spec(TB, D)] + [_const_spec(a.shape) for a in ins[1:]]
    outs = [(d_mix, F32), (d_mix, F32), (q_lora, BF16), (kv_lora, BF16), (HEAD_GROUP, F32)]
    return pl.pallas_call(
        body,
        grid=(B, S // TB),
        in_specs=in_specs,
        out_specs=[_tok_spec(TB, w) for w, _ in outs],
        out_shape=[jax.ShapeDtypeStruct((B, S, w), dt) for w, dt in outs],
        scratch_shapes=[pltpu.VMEM(((CONV_WIDTH - 1) * SUBLANES, d_rnn), F32), pltpu.VMEM((1, d_rnn), F32)],
        compiler_params=pltpu.CompilerParams(
            dimension_semantics=("arbitrary", "arbitrary"), vmem_limit_bytes=VMEM_LIMIT),
        name="in_proj_rglru",
    )(*ins)


def _qkv_body(qn_ref, kvn_ref, kpe_ref, cos_ref, sin_ref, wq_ref, wk_ref, wvt_ref, vbias_ref, q_ref, k_ref, vt_ref):
    cosv = cos_ref[...]
    sinv = sin_ref[...]

    def rope(z):
        return z * cosv + pltpu.roll(z, HALF_GROUP, 1) * sinv

    kpe = rope(kpe_ref[...])
    kvn = kvn_ref[...]
    qf = _dot(qn_ref[...], wq_ref[...])
    kf = _dot(kvn, wk_ref[...])
    for h in range(qf.shape[1] // HEAD_GROUP):
        sl = slice(h * HEAD_GROUP, (h + 1) * HEAD_GROUP)
        q_ref[:, sl] = (rope(qf[:, sl]) * SCORE_SCALE).astype(BF16)
        k_ref[:, sl] = (kf[:, sl] + kpe).astype(BF16)
    vt = lax.dot_general(wvt_ref[...], kvn, (((1,), (1,)), ((), ())), preferred_element_type=F32)
    vbias = vbias_ref[...]
    for c in range(vt.shape[1] // LANES):
        sl = slice(c * LANES, (c + 1) * LANES)
        vt_ref[:, sl] = (vt[:, sl] + vbias).astype(BF16)


def _qkv_up(qn, kvn, kpe, cos_t, sin_t, wq, wk, wvt, vbias):
    B, S, _ = qn.shape
    ins = [qn, kvn, kpe, cos_t, sin_t, wq, wk, wvt, vbias]
    in_specs = [_tok_spec(TB, a.shape[2]) for a in ins[:5]] + [_const_spec(a.shape) for a in ins[5:]]
    rows = wvt.shape[0]
    return pl.pallas_call(
        _qkv_body,
        grid=(B, S // TB),
        in_specs=in_specs,
        out_specs=[_tok_spec(TB, wq.shape[1]), _tok_spec(TB, wk.shape[1]),
                   pl.BlockSpec((None, None, rows, TB), lambda b, s: (b, s, 0, 0))],
        out_shape=[jax.ShapeDtypeStruct((B, S, wq.shape[1]), BF16),
                   jax.ShapeDtypeStruct((B, S, wk.shape[1]), BF16),
                   jax.ShapeDtypeStruct((B, S // TB, rows, TB), BF16)],
        compiler_params=pltpu.CompilerParams(
            dimension_semantics=("parallel", "parallel"), vmem_limit_bytes=VMEM_LIMIT),
        name="qkv_up",
    )(*ins)


def _attn_body(alpha, q_ref, k_ref, vt_ref, pa_ref, gb_ref, x_ref, wo_ref, g_ref, b_ref, o_ref,
               s_ref, m_ref, acc_ref):
    n_heads = m_ref.shape[0]
    qi = pl.program_id(1)
    key_t = _strand_time(lax.broadcasted_iota(jnp.int32, (TB, TB), 0))
    qry_t = _strand_time(lax.broadcasted_iota(jnp.int32, (TB, TB), 1))
    causal = key_t <= qry_t
    m_ref[...] = jnp.full(m_ref.shape, NEG_INF, F32)
    acc_ref[...] = jnp.zeros(acc_ref.shape, F32)

    def scores(j, h):
        ksl = slice(h * HEAD_GROUP, (h + 1) * HEAD_GROUP)
        kj = k_ref[pl.ds(pl.multiple_of(j * TB, TB), TB), ksl]
        return lax.dot_general(kj, q_ref[:, ksl], (((1,), (1,)), ((), ())), preferred_element_type=F32)

    def attend(j, h, s):
        m_old = m_ref[h]
        m_new = jnp.maximum(m_old, jnp.max(s, axis=0, keepdims=True))
        p = jnp.exp2(s - m_new).astype(BF16)
        pv = _dot(vt_ref[j, h * V_ROWS:(h + 1) * V_ROWS, :], p)
        acc_ref[h] = acc_ref[h] * jnp.exp2(m_old - m_new) + pv
        m_ref[h] = m_new

    for h in range(n_heads):
        s_ref[h] = scores(0, h)

    def body(j, carry):
        for h in range(n_heads):
            s = s_ref[h]
            s_ref[h] = scores(j + 1, h)
            attend(j, h, s)
        return carry

    lax.fori_loop(0, qi, body, 0)
    for h in range(n_heads):
        attend(qi, h, jnp.where(causal, s_ref[h], NEG_INF))

    ys = []
    for h in range(n_heads):
        a = acc_ref[h]
        ys.append(a[:V_HEAD] / a[V_HEAD:V_HEAD + 1])
    y = jnp.concatenate(ys, axis=0).T
    merged = pa_ref[...] + _sigmoid(gb_ref[...]) * y
    o = _dot(merged.astype(BF16), wo_ref[...])
    o_ref[...] = _layer_norm(alpha * x_ref[...] + o, g_ref[...], b_ref[...])


def _attn_out(q, k, vt, pa, gb, x, w_out, g, b, *, alpha):
    B, S, D = x.shape
    n_heads = k.shape[2] // HEAD_GROUP
    assert vt.shape[3] == TB and vt.shape[2] == n_heads * V_ROWS
    in_specs = [_tok_spec(TB, q.shape[2]),
                pl.BlockSpec((None, S, k.shape[2]), lambda bi, s: (bi, 0, 0)),
                pl.BlockSpec((None,) + vt.shape[1:], lambda bi, s: (bi, 0, 0, 0)),
                _tok_spec(TB, pa.shape[2]), _tok_spec(TB, gb.shape[2]), _tok_spec(TB, D),
                _const_spec(w_out.shape), _const_spec(g.shape), _const_spec(b.shape)]
    return pl.pallas_call(
        functools.partial(_attn_body, alpha),
        grid=(B, S // TB),
        in_specs=in_specs,
        out_specs=_tok_spec(TB, D),
        out_shape=jax.ShapeDtypeStruct((B, S, D), F32),
        scratch_shapes=[pltpu.VMEM((n_heads, TB, TB), F32), pltpu.VMEM((n_heads, 1, TB), F32),
                        pltpu.VMEM((n_heads, V_ROWS, TB), F32)],
        compiler_params=pltpu.CompilerParams(
            dimension_semantics=("parallel", "arbitrary"), vmem_limit_bytes=VMEM_LIMIT),
        name="attn_out",
    )(q, k, vt, pa, gb, x, w_out, g, b)


def _ffn_body(alpha, chunk, x_ref, wu_ref, cw_ref, cb_ref, wd_ref, g_ref, b_ref, o_ref, hprev_ref):
    @pl.when(pl.program_id(1) == 0)
    def _():
        hprev_ref[...] = jnp.zeros_like(hprev_ref)

    tail = hprev_ref.shape[0]
    d_ff = wd_ref.shape[0]
    x = x_ref[...]
    xb = x.astype(BF16)
    acc = jnp.zeros(o_ref.shape, F32)
    for c in range(d_ff // chunk):
        cs = slice(2 * c * chunk, 2 * (c + 1) * chunk)
        hcur = _dot(xb, wu_ref[:, cs])
        hconv = _causal_conv(hprev_ref[:, cs], hcur, cw_ref, cb_ref, cs, FFN_CONV_WIDTH)
        hprev_ref[:, cs] = hcur[TB - tail:]
        act = (_gelu_tanh(hconv[:, :chunk]) * hconv[:, chunk:]).astype(BF16)
        acc = acc + _dot(act, wd_ref[c * chunk:(c + 1) * chunk, :])
    o_ref[...] = _layer_norm(alpha * x + acc, g_ref[...], b_ref[...])


def _conv_ffn(x, w_up, cw, cb, w_down, g, b, *, chunk, alpha):
    B, S, D = x.shape
    ins = [x, w_up, cw, cb, w_down, g, b]
    return pl.pallas_call(
        functools.partial(_ffn_body, alpha, chunk),
        grid=(B, S // TB),
        in_specs=[_tok_spec(TB, D)] + [_const_spec(a.shape) for a in ins[1:]],
        out_specs=_tok_spec(TB, D),
        out_shape=jax.ShapeDtypeStruct((B, S, D), F32),
        scratch_shapes=[pltpu.VMEM(((FFN_CONV_WIDTH - 1) * SUBLANES, w_up.shape[1]), F32)],
        compiler_params=pltpu.CompilerParams(
            dimension_semantics=("arbitrary", "arbitrary"), vmem_limit_bytes=VMEM_LIMIT),
        name="conv_ffn",
    )(*ins)


def _block_diag(w):
    H, r, _ = w.shape
    eye = jnp.eye(H, dtype=w.dtype)
    return (eye[:, None, :, None] * w[:, :, None, :]).reshape(H * r, H * r)


def _interleave_halves(a, chunk):
    lead = a.shape[:-1]
    f = a.shape[-1] // 2
    return a.reshape(*lead, 2, f // chunk, chunk).swapaxes(-3, -2).reshape(*lead, 2 * f)


def _head_group(nope, rope):
    ref = nope if nope is not None else rope
    z = lambda n: jnp.zeros(ref.shape[:-1] + (n,), ref.dtype)
    r0, r1 = (z(ROPE_HALF), z(ROPE_HALF)) if rope is None else (rope[..., :ROPE_HALF], rope[..., ROPE_HALF:])
    split = HALF_GROUP - ROPE_HALF
    n0, n1 = (z(split), z(QK_NOPE - split)) if nope is None else (nope[..., :split], nope[..., split:])
    return jnp.concatenate([r0, n0, r1, n1, z(HEAD_GROUP - QK_NOPE - QK_ROPE)], axis=-1)


def _layer_weights(l, rnn_chunk, ffn_chunk, w_in, conv_w, conv_b, gx_w, gx_b, ga_w, ga_b, lru_lambda,
                   q_norm_g, w_uq, kv_norm_g, w_ukv, w_out, ln1_g, ln1_b,
                   w_up, ffn_conv_w, ffn_conv_b, w_down, ln2_g, ln2_b):
    d_model = w_in.shape[1]
    d_rnn = conv_w.shape[2]
    q_lora, kv_lora = w_uq.shape[1], w_ukv.shape[1]
    p_q = 2 * d_rnn
    p_kv = p_q + q_lora
    p_kr = p_kv + kv_lora
    p_a = p_kr + QK_ROPE
    w = w_in[l]
    w_in_p = jnp.concatenate(
        [w[:, :p_q], w[:, p_a:], w[:, p_q:p_kr], _head_group(None, w[:, p_kr:p_a])], axis=1).astype(BF16)
    gx_bd, ga_bd = _block_diag(gx_w[l]), _block_diag(ga_w[l])
    ch = rnn_chunk
    wg = jnp.stack([
        jnp.concatenate([gx_bd[c * ch:(c + 1) * ch, c * ch:(c + 1) * ch],
                         ga_bd[c * ch:(c + 1) * ch, c * ch:(c + 1) * ch]], axis=1)
        for c in range(d_rnn // ch)]).astype(BF16)
    row = lambda a: a[l].reshape(1, -1)
    wq = w_uq[l].reshape(q_lora, N_HEADS, QK_NOPE + QK_ROPE)
    wq = _head_group(wq[:, :, :QK_NOPE], wq[:, :, QK_NOPE:]).reshape(q_lora, N_HEADS * HEAD_GROUP).astype(BF16)
    wkv = w_ukv[l].reshape(kv_lora, N_HEADS, QK_NOPE + V_HEAD)
    wk = _head_group(wkv[:, :, :QK_NOPE], None).reshape(kv_lora, N_HEADS * HEAD_GROUP).astype(BF16)
    wvt = jnp.pad(jnp.transpose(wkv[:, :, QK_NOPE:], (1, 2, 0)), ((0, 0), (0, V_ROWS - V_HEAD), (0, 0)))
    wvt = wvt.reshape(N_HEADS * V_ROWS, kv_lora).astype(BF16)
    return dict(
        w_in_p=w_in_p, conv_w=conv_w[l], conv_b=row(conv_b), wg=wg, gx_b=row(gx_b), ga_b=row(ga_b),
        lam=row(lru_lambda), qg=row(q_norm_g), kvg=row(kv_norm_g), wq=wq, wk=wk, wvt=wvt,
        w_out=w_out[l].astype(BF16), ln1_g=row(ln1_g), ln1_b=row(ln1_b),
        w_up=_interleave_halves(w_up[l], ffn_chunk).astype(BF16),
        fcw=_interleave_halves(ffn_conv_w[l], ffn_chunk), fcb=_interleave_halves(row(ffn_conv_b), ffn_chunk),
        w_down=w_down[l].astype(BF16), ln2_g=row(ln2_g), ln2_b=row(ln2_b))


def kernel(x, positions, w_in, conv_w, conv_b, gx_w, gx_b, ga_w, ga_b, lru_lambda, q_norm_g, w_uq, kv_norm_g, w_ukv, w_out, ln1_g, ln1_b, w_up, ffn_conv_w, ffn_conv_b, w_down, ln2_g, ln2_b):
    depth = w_in.shape[0]
    assert x.shape[1] % TB == 0
    alpha = (2 * depth) ** 0.25
    rnn_chunk = 256
    ffn_chunk = 512
    x = _to_strands(x)
    cos_t, sin_t = _rope_tables(_to_strands(positions), TB)
    ones_row = (np.arange(N_HEADS * V_ROWS) % V_ROWS == V_HEAD).astype(np.float32)
    vbias = jnp.asarray(np.repeat(ones_row[:, None], LANES, axis=1))
    for l in range(depth):
        p = _layer_weights(l, rnn_chunk, ffn_chunk, w_in, conv_w, conv_b, gx_w, gx_b, ga_w, ga_b, lru_lambda,
                           q_norm_g, w_uq, kv_norm_g, w_ukv, w_out, ln1_g, ln1_b,
                           w_up, ffn_conv_w, ffn_conv_b, w_down, ln2_g, ln2_b)
        pa, gb, qn, kvn, kpe = _in_proj_rglru(
            x, p["w_in_p"], p["conv_w"], p["conv_b"], p["wg"], p["gx_b"], p["ga_b"], p["lam"],
            p["qg"], p["kvg"], chunk=rnn_chunk)
        q, k, vt = _qkv_up(qn, kvn, kpe, cos_t, sin_t, p["wq"], p["wk"], p["wvt"], vbias)
        x = _attn_out(q, k, vt, pa, gb, x, p["w_out"], p["ln1_g"], p["ln1_b"], alpha=alpha)
        x = _conv_ffn(x, p["w_up"], p["fcw"], p["fcb"], p["w_down"], p["ln2_g"], p["ln2_b"],
                      chunk=ffn_chunk, alpha=alpha)
    return _from_strands(x)
```

```python
import functools
import math

import jax
import jax.numpy as jnp
import numpy as np
from jax import lax
from jax.experimental import pallas as pl
from jax.experimental.pallas import tpu as pltpu

RNN_BLOCKS = 16
CONV_WIDTH = 4
LRU_C = 8.0
N_HEADS = 16
QK_NOPE = 64
QK_ROPE = 32
V_HEAD = 64
ROPE_THETA = 10000.0
FFN_CONV_WIDTH = 3
EPS = 1e-6
NEG_INF = -1e30

LANES = 128
SUBLANES = 8
TB = 256
ROWS = TB // SUBLANES
HEAD_GROUP = LANES
ROPE_HALF = QK_ROPE // 2
HALF_GROUP = HEAD_GROUP // 2
V_ROWS = 80
VMEM_LIMIT = 56 * 1024 * 1024
SCORE_SCALE = (QK_NOPE + QK_ROPE) ** -0.5 * math.log2(math.e)

BF16 = jnp.bfloat16
F32 = jnp.float32


def _dot(a, b):
    return jnp.dot(a, b, preferred_element_type=F32)


def _gelu_tanh(x):
    c = math.sqrt(2.0 / math.pi)
    return x * (0.5 * (1.0 + jnp.tanh(c * (x + 0.044715 * (x * x * x)))))


def _sigmoid(x):
    return 0.5 * jnp.tanh(0.5 * x) + 0.5


def _sqrt_nonneg(x):
    return jnp.where(x > 0.0, x * lax.rsqrt(x), 0.0)


def _layer_norm(z, g, b):
    mu = jnp.mean(z, axis=-1, keepdims=True)
    zc = z - mu
    var = jnp.mean(zc * zc, axis=-1, keepdims=True)
    return zc * lax.rsqrt(var + EPS) * g + b


def _rms_norm(z, g):
    return z * lax.rsqrt(jnp.mean(z * z, axis=-1, keepdims=True) + EPS) * g


def _to_strands(a):
    B, S = a.shape[:2]
    rest = a.shape[2:]
    return a.reshape(B, S // TB, SUBLANES, ROWS, *rest).swapaxes(2, 3).reshape(B, S, *rest)


def _from_strands(a):
    B, S = a.shape[:2]
    rest = a.shape[2:]
    return a.reshape(B, S // TB, ROWS, SUBLANES, *rest).swapaxes(2, 3).reshape(B, S, *rest)


def _strand_time(r):
    return (r & (SUBLANES - 1)) * ROWS + (r >> 3)


def _delays(prev_tail, x, depth):
    n = depth * SUBLANES
    C = x.shape[1]
    sub = lax.broadcasted_iota(jnp.int32, (depth, SUBLANES, C), 1)
    own = pltpu.roll(x[TB - n:].reshape(depth, SUBLANES, C), 1, 1)
    prev = pltpu.roll(prev_tail.reshape(depth, SUBLANES, C), 1, 1)
    head = jnp.where(sub == 0, prev, own).reshape(n, C)
    ext = jnp.concatenate([head, x], axis=0)
    return [ext[(depth - k) * SUBLANES:(depth - k) * SUBLANES + TB] for k in range(1, depth + 1)]


def _blocks(z):
    return [z[i * TB:(i + 1) * TB] for i in range(z.shape[0] // TB)]


def _causal_conv(prev_ref, z, w_ref, b_ref, cs, width):
    tail = prev_ref.shape[0]
    prev = prev_ref[:, cs]
    outs = []
    for x in _blocks(z):
        out = x * w_ref[width - 1:width, cs] + b_ref[:, cs]
        for k, xd in enumerate(_delays(prev, x, width - 1), start=1):
            out = out + xd * w_ref[width - 1 - k:width - k, cs]
        outs.append(out)
        prev = x[TB - tail:]
    prev_ref[:, cs] = prev
    return jnp.concatenate(outs, axis=0)


def _layer_spec(a, l):
    nd = a.ndim - 1
    return pl.BlockSpec((None,) + a.shape[1:], lambda *_: (l,) + (0,) * nd, pipeline_mode=pl.Buffered(1))


def _tok_spec(tm, width):
    return pl.BlockSpec((None, tm, width), lambda b, s: (b, s, 0))


def _rope_tables_body(pos_ref, invf_ref, cos_ref, sin_ref):
    pos = pos_ref[...].astype(F32)
    ang = pos * invf_ref[...]
    lane = lax.broadcasted_iota(jnp.int32, ang.shape, 1)
    cosv = jnp.cos(ang)
    sinv = jnp.sin(ang)
    rot = (lane & (HALF_GROUP - 1)) < ROPE_HALF
    cos_ref[...] = jnp.where(rot, cosv, 1.0)
    sin_ref[...] = jnp.where(rot, jnp.where(lane < HALF_GROUP, -sinv, sinv), 0.0)


def _rope_tables(positions, tm):
    B, S = positions.shape
    inv_freq = ROPE_THETA ** (-jnp.arange(0, QK_ROPE, 2, dtype=F32) / QK_ROPE)
    idx = np.minimum(np.arange(LANES) % HALF_GROUP, ROPE_HALF - 1)
    invf = inv_freq[idx].reshape(1, LANES)
    pos = positions.reshape(B, S, 1)
    out = jax.ShapeDtypeStruct((B, S, LANES), F32)
    return pl.pallas_call(
        _rope_tables_body,
        grid=(B, S // tm),
        in_specs=[_tok_spec(tm, 1), pl.BlockSpec((1, LANES), lambda b, s: (0, 0))],
        out_specs=[_tok_spec(tm, LANES), _tok_spec(tm, LANES)],
        out_shape=[out, out],
        compiler_params=pltpu.CompilerParams(dimension_semantics=("parallel", "parallel")),
        name="rope_tables",
    )(pos, invf)


def _lru_scan(a, u, h0):
    C = a.shape[1]
    row = lambda z, v: z[v * SUBLANES:(v + 1) * SUBLANES]
    p, h = row(a, 0), row(u, 0)
    ps, hs = [p], [h]
    for v in range(1, ROWS):
        av = row(a, v)
        p = av * p
        h = av * h + row(u, v)
        ps.append(p)
        hs.append(h)
    sub = lax.broadcasted_iota(jnp.int32, (SUBLANES, C), 0)
    pc, hc = p, h
    step = 1
    while step < SUBLANES:
        m = sub >= step
        rp = pltpu.roll(pc, step, 0)
        rh = pltpu.roll(hc, step, 0)
        hc = jnp.where(m, pc * rh + hc, hc)
        pc = jnp.where(m, pc * rp, pc)
        step *= 2
    end = hc + pc * h0
    init = jnp.where(sub == 0, h0, pltpu.roll(end, 1, 0))
    out = jnp.concatenate([hs[v] + ps[v] * init for v in range(ROWS)], axis=0)
    return out, end[SUBLANES - 1:]


def _in_proj_body(d_rnn, d_mix, q_lora, kv_lora, chunk,
                  x_ref, w_ref, cw_ref, cb_ref, wg_ref, gxb_ref, gab_ref, lam_ref, qg_ref, kvg_ref,
                  pa_ref, gb_ref, qn_ref, kvn_ref, kpe_ref, xprev_ref, hc_ref):
    @pl.when(pl.program_id(1) == 0)
    def _():
        xprev_ref[...] = jnp.zeros_like(xprev_ref)
        hc_ref[...] = jnp.zeros_like(hc_ref)

    xb = x_ref[...].astype(BF16)
    neg_lam = -lam_ref[...]
    softplus = jnp.maximum(neg_lam, 0.0) + jnp.log1p(jnp.exp(-jnp.abs(neg_lam)))
    off_g, off_a, off_b = d_rnn, 2 * d_rnn, 2 * d_rnn + d_mix
    off_q = 2 * d_rnn + 2 * d_mix
    off_kv = off_q + q_lora
    off_kpe = off_kv + kv_lora

    for c in range(d_rnn // chunk):
        cs = slice(c * chunk, (c + 1) * chunk)
        xr = _dot(xb, w_ref[:, cs])
        conv = _causal_conv(xprev_ref, xr, cw_ref, cb_ref, cs, CONV_WIDTH)

        gates = _dot(conv.astype(BF16), wg_ref[c])
        gate_x = _sigmoid(gates[:, :chunk] + gxb_ref[:, cs])
        gate_r = _sigmoid(gates[:, chunk:] + gab_ref[:, cs])
        log_a = (-LRU_C) * gate_r * softplus[:, cs]
        a = jnp.exp(log_a)
        t = jnp.tanh(log_a)
        mult = _sqrt_nonneg((-2.0 * t) / (1.0 - t))
        u = mult * (gate_x * conv)
        hs, hlast = [], hc_ref[:, cs]
        for a_blk, u_blk in zip(_blocks(a), _blocks(u)):
            h_blk, hlast = _lru_scan(a_blk, u_blk, hlast)
            hs.append(h_blk)
        h = jnp.concatenate(hs, axis=0)
        hc_ref[:, cs] = hlast

        g_rnn = _dot(xb, w_ref[:, off_g + c * chunk:off_g + (c + 1) * chunk])
        y_rnn = _gelu_tanh(g_rnn) * h
        gate_a = _dot(xb, w_ref[:, off_a + c * chunk:off_a + (c + 1) * chunk])
        pa_ref[:, cs] = _sigmoid(gate_a) * y_rnn
        gb_ref[:, cs] = _dot(xb, w_ref[:, off_b + c * chunk:off_b + (c + 1) * chunk])

    q_lat = _dot(xb, w_ref[:, off_q:off_kv])
    qn_ref[...] = _rms_norm(q_lat, qg_ref[...]).astype(BF16)
    kv_lat = _dot(xb, w_ref[:, off_kv:off_kpe])
    kvn_ref[...] = _rms_norm(kv_lat, kvg_ref[...]).astype(BF16)
    kpe_ref[...] = _dot(xb, w_ref[:, off_kpe:off_kpe + HEAD_GROUP])


def _in_proj_rglru(l, x, w_in_p, conv_w, conv_b, wg, gx_b, ga_b, lam, qg, kvg, *, tm, chunk):
    B, S, D = x.shape
    d_rnn = conv_w.shape[2]
    q_lora, kv_lora = qg.shape[2], kvg.shape[2]
    d_mix = (w_in_p.shape[2] - 2 * d_rnn - q_lora - kv_lora - HEAD_GROUP) // 2
    body = functools.partial(_in_proj_body, d_rnn, d_mix, q_lora, kv_lora, chunk)
    ins = [x, w_in_p, conv_w, conv_b, wg, gx_b, ga_b, lam, qg, kvg]
    in_specs = [_tok_spec(tm, D)] + [_layer_spec(a, l) for a in ins[1:]]
    outs = [(d_mix, F32), (d_mix, F32), (q_lora, BF16), (kv_lora, BF16), (HEAD_GROUP, F32)]
    return pl.pallas_call(
        body,
        grid=(B, S // tm),
        in_specs=in_specs,
        out_specs=[_tok_spec(tm, w) for w, _ in outs],
        out_shape=[jax.ShapeDtypeStruct((B, S, w), dt) for w, dt in outs],
        scratch_shapes=[pltpu.VMEM(((CONV_WIDTH - 1) * SUBLANES, d_rnn), F32), pltpu.VMEM((1, d_rnn), F32)],
        compiler_params=pltpu.CompilerParams(
            dimension_semantics=("arbitrary", "arbitrary"), vmem_limit_bytes=VMEM_LIMIT),
        name="in_proj_rglru",
    )(*ins)


def _qkv_body(qn_ref, kvn_ref, kpe_ref, cos_ref, sin_ref, wq_ref, wk_ref, wvt_ref, vbias_ref, q_ref, k_ref, vt_ref):
    cosv = cos_ref[...]
    sinv = sin_ref[...]

    def rope(z):
        return z * cosv + pltpu.roll(z, HALF_GROUP, 1) * sinv

    kpe = rope(kpe_ref[...])
    kvn = kvn_ref[...]
    qf = _dot(qn_ref[...], wq_ref[...])
    kf = _dot(kvn, wk_ref[...])
    for h in range(qf.shape[1] // HEAD_GROUP):
        sl = slice(h * HEAD_GROUP, (h + 1) * HEAD_GROUP)
        q_ref[:, sl] = (rope(qf[:, sl]) * SCORE_SCALE).astype(BF16)
        k_ref[:, sl] = (kf[:, sl] + kpe).astype(BF16)
    vt = lax.dot_general(wvt_ref[...], kvn, (((1,), (1,)), ((), ())), preferred_element_type=F32)
    vbias = vbias_ref[...]
    for c in range(vt.shape[1] // LANES):
        sl = slice(c * LANES, (c + 1) * LANES)
        vt_ref[:, sl] = (vt[:, sl] + vbias).astype(BF16)


def _qkv_up(l, qn, kvn, kpe, cos_t, sin_t, wq, wk, wvt, vbias):
    B, S, _ = qn.shape
    ins = [qn, kvn, kpe, cos_t, sin_t, wq, wk, wvt, vbias]
    in_specs = ([_tok_spec(TB, a.shape[2]) for a in ins[:5]] + [_layer_spec(a, l) for a in ins[5:8]]
                + [pl.BlockSpec(vbias.shape, lambda b, s: (0, 0), pipeline_mode=pl.Buffered(1))])
    rows = wvt.shape[1]
    return pl.pallas_call(
        _qkv_body,
        grid=(B, S // TB),
        in_specs=in_specs,
        out_specs=[_tok_spec(TB, wq.shape[2]), _tok_spec(TB, wk.shape[2]),
                   pl.BlockSpec((None, None, rows, TB), lambda b, s: (b, s, 0, 0))],
        out_shape=[jax.ShapeDtypeStruct((B, S, wq.shape[2]), BF16),
                   jax.ShapeDtypeStruct((B, S, wk.shape[2]), BF16),
                   jax.ShapeDtypeStruct((B, S // TB, rows, TB), BF16)],
        compiler_params=pltpu.CompilerParams(
            dimension_semantics=("parallel", "parallel"), vmem_limit_bytes=VMEM_LIMIT),
        name="qkv_up",
    )(*ins)


def _attn_body(alpha, q_ref, k_ref, vt_ref, qn_ref, kn_ref, pa_ref, gb_ref, x_ref, wo_ref, g_ref, b_ref, o_ref,
               s_ref, m_ref, acc_ref):
    n_heads = m_ref.shape[0]
    qi = pl.program_id(1)
    key_t = _strand_time(lax.broadcasted_iota(jnp.int32, (TB, TB), 0))
    qry_t = _strand_time(lax.broadcasted_iota(jnp.int32, (TB, TB), 1))
    mask_bias = jnp.where(key_t <= qry_t, 0.0, NEG_INF)
    m_ref[...] = jnp.full(m_ref.shape, NEG_INF, F32)
    acc_ref[...] = jnp.zeros(acc_ref.shape, F32)
    head = lambda h: slice(h * HEAD_GROUP, (h + 1) * HEAD_GROUP)

    def scores(k_blk, q_blk):
        return lax.dot_general(k_blk, q_blk, (((1,), (1,)), ((), ())), preferred_element_type=F32)

    def attend(j, h, s):
        m_old = m_ref[h]
        m_new = jnp.maximum(m_old, jnp.max(s, axis=0, keepdims=True))
        p = jnp.exp2(s - m_new).astype(BF16)
        pv = _dot(vt_ref[j, h * V_ROWS:(h + 1) * V_ROWS, :], p)
        acc_ref[h] = acc_ref[h] * jnp.exp2(m_old - m_new) + pv
        m_ref[h] = m_new

    @pl.when((pl.program_id(0) == 0) & (qi == 0))
    def _():
        for h in range(n_heads):
            s_ref[h] = scores(k_ref[0:TB, head(h)], q_ref[:, head(h)])

    def body(j, carry):
        off = pl.multiple_of((j + 1) * TB, TB)
        for h in range(n_heads):
            s = s_ref[h]
            s_ref[h] = scores(k_ref[pl.ds(off, TB), head(h)], q_ref[:, head(h)])
            attend(j, h, s)
        return carry

    lax.fori_loop(0, qi, body, 0)
    for h in range(n_heads):
        s = s_ref[h]
        s_ref[h] = scores(kn_ref[:, head(h)], qn_ref[:, head(h)])
        attend(qi, h, s + mask_bias)

    ys = []
    for h in range(n_heads):
        a = acc_ref[h]
        ys.append(a[:V_HEAD] / a[V_HEAD:V_HEAD + 1])
    y = jnp.concatenate(ys, axis=0).T
    merged = pa_ref[...] + _sigmoid(gb_ref[...]) * y
    o = _dot(merged.astype(BF16), wo_ref[...])
    o_ref[...] = _layer_norm(alpha * x_ref[...] + o, g_ref[...], b_ref[...])


def _attn_out(l, q, k, vt, pa, gb, x, w_out, g, b, *, alpha):
    B, S, D = x.shape
    n_heads = k.shape[2] // HEAD_GROUP
    n_steps = S // TB
    assert vt.shape[3] == TB and vt.shape[2] == n_heads * V_ROWS

    def next_step(bi, s):
        wrap = s + 1 == n_steps
        return jnp.where(wrap, jnp.minimum(bi + 1, B - 1), bi), jnp.where(wrap, 0, s + 1)

    in_specs = [_tok_spec(TB, q.shape[2]),
                pl.BlockSpec((None, S, k.shape[2]), lambda bi, s: (bi, 0, 0)),
                pl.BlockSpec((None,) + vt.shape[1:], lambda bi, s: (bi, 0, 0, 0)),
                pl.BlockSpec((None, TB, q.shape[2]), lambda bi, s: (*next_step(bi, s), 0)),
                pl.BlockSpec((None, TB, k.shape[2]), lambda bi, s: (next_step(bi, s)[0], 0, 0)),
                _tok_spec(TB, pa.shape[2]), _tok_spec(TB, gb.shape[2]), _tok_spec(TB, D),
                _layer_spec(w_out, l), _layer_spec(g, l), _layer_spec(b, l)]
    return pl.pallas_call(
        functools.partial(_attn_body, alpha),
        grid=(B, n_steps),
        in_specs=in_specs,
        out_specs=_tok_spec(TB, D),
        out_shape=jax.ShapeDtypeStruct((B, S, D), F32),
        scratch_shapes=[pltpu.VMEM((n_heads, TB, TB), F32), pltpu.VMEM((n_heads, 1, TB), F32),
                        pltpu.VMEM((n_heads, V_ROWS, TB), F32)],
        compiler_params=pltpu.CompilerParams(
            dimension_semantics=("arbitrary", "arbitrary"), vmem_limit_bytes=VMEM_LIMIT),
        name="attn_out",
    )(q, k, vt, q, k, pa, gb, x, w_out, g, b)


def _ffn_body(alpha, chunk, x_ref, wu_ref, cw_ref, cb_ref, wd_ref, g_ref, b_ref, o_ref, hprev_ref):
    @pl.when(pl.program_id(1) == 0)
    def _():
        hprev_ref[...] = jnp.zeros_like(hprev_ref)

    d_ff = wd_ref.shape[0]
    k0 = math.sqrt(2.0 / math.pi)
    x = x_ref[...]
    xb = x.astype(BF16)
    acc = jnp.zeros(o_ref.shape, F32)
    for c in range(d_ff // chunk):
        cs = slice(2 * c * chunk, 2 * (c + 1) * chunk)
        hconv = _causal_conv(hprev_ref, _dot(xb, wu_ref[:, cs]), cw_ref, cb_ref, cs, FFN_CONV_WIDTH)
        hg, half_hv = hconv[:, :chunk], hconv[:, chunk:]
        inner = hg * (k0 + (k0 * 0.044715) * (hg * hg))
        act = ((hg * half_hv) * (1.0 + jnp.tanh(inner))).astype(BF16)
        acc = acc + _dot(act, wd_ref[c * chunk:(c + 1) * chunk, :])
    o_ref[...] = _layer_norm(alpha * x + acc, g_ref[...], b_ref[...])


def _conv_ffn(l, x, w_up, cw, cb, w_down, g, b, *, tm, chunk, alpha):
    B, S, D = x.shape
    ins = [x, w_up, cw, cb, w_down, g, b]
    return pl.pallas_call(
        functools.partial(_ffn_body, alpha, chunk),
        grid=(B, S // tm),
        in_specs=[_tok_spec(tm, D)] + [_layer_spec(a, l) for a in ins[1:]],
        out_specs=_tok_spec(tm, D),
        out_shape=jax.ShapeDtypeStruct((B, S, D), F32),
        scratch_shapes=[pltpu.VMEM(((FFN_CONV_WIDTH - 1) * SUBLANES, w_up.shape[2]), F32)],
        compiler_params=pltpu.CompilerParams(
            dimension_semantics=("arbitrary", "arbitrary"), vmem_limit_bytes=VMEM_LIMIT),
        name="conv_ffn",
    )(*ins)


def _block_diag(w):
    L, H, r, _ = w.shape
    eye = jnp.eye(H, dtype=w.dtype)
    return (eye[:, None, :, None] * w[:, :, :, None, :]).reshape(L, H * r, H * r)


def _interleave_halves(a, chunk):
    lead = a.shape[:-1]
    f = a.shape[-1] // 2
    return a.reshape(*lead, 2, f // chunk, chunk).swapaxes(-3, -2).reshape(*lead, 2 * f)


def _head_group(nope, rope):
    ref = nope if nope is not None else rope
    z = lambda n: jnp.zeros(ref.shape[:-1] + (n,), ref.dtype)
    r0, r1 = (z(ROPE_HALF), z(ROPE_HALF)) if rope is None else (rope[..., :ROPE_HALF], rope[..., ROPE_HALF:])
    split = HALF_GROUP - ROPE_HALF
    n0, n1 = (z(split), z(QK_NOPE - split)) if nope is None else (nope[..., :split], nope[..., split:])
    return jnp.concatenate([r0, n0, r1, n1, z(HEAD_GROUP - QK_NOPE - QK_ROPE)], axis=-1)


def _prepare_weights(rnn_chunk, ffn_chunk, w_in, conv_w, conv_b, gx_w, gx_b, ga_w, ga_b, lru_lambda,
                     q_norm_g, w_uq, kv_norm_g, w_ukv, w_out, ln1_g, ln1_b,
                     w_up, ffn_conv_w, ffn_conv_b, w_down, ln2_g, ln2_b):
    L = w_in.shape[0]
    d_rnn = conv_w.shape[2]
    q_lora, kv_lora = w_uq.shape[1], w_ukv.shape[1]
    d_ff = w_down.shape[1]
    p_q = 2 * d_rnn
    p_kv = p_q + q_lora
    p_kr = p_kv + kv_lora
    p_a = p_kr + QK_ROPE
    w_in_p = jnp.concatenate(
        [w_in[..., :p_q], w_in[..., p_a:], w_in[..., p_q:p_kr], _head_group(None, w_in[..., p_kr:p_a])],
        axis=-1).astype(BF16)
    gx_bd, ga_bd = _block_diag(gx_w), _block_diag(ga_w)
    ch = rnn_chunk
    wg = jnp.stack([
        jnp.concatenate([gx_bd[:, c * ch:(c + 1) * ch, c * ch:(c + 1) * ch],
                         ga_bd[:, c * ch:(c + 1) * ch, c * ch:(c + 1) * ch]], axis=-1)
        for c in range(d_rnn // ch)], axis=1).astype(BF16)
    row = lambda a: a[:, None, :]
    wq = w_uq.reshape(L, q_lora, N_HEADS, QK_NOPE + QK_ROPE)
    wq = _head_group(wq[..., :QK_NOPE], wq[..., QK_NOPE:]).reshape(L, q_lora, N_HEADS * HEAD_GROUP).astype(BF16)
    wkv = w_ukv.reshape(L, kv_lora, N_HEADS, QK_NOPE + V_HEAD)
    wk = _head_group(wkv[..., :QK_NOPE], None).reshape(L, kv_lora, N_HEADS * HEAD_GROUP).astype(BF16)
    wvt = jnp.pad(jnp.transpose(wkv[..., QK_NOPE:], (0, 2, 3, 1)), ((0, 0), (0, 0), (0, V_ROWS - V_HEAD), (0, 0)))
    wvt = wvt.reshape(L, N_HEADS * V_ROWS, kv_lora).astype(BF16)
    val_half = jnp.concatenate([jnp.ones((d_ff,), F32), jnp.full((d_ff,), 0.5, F32)])
    return dict(
        w_in_p=w_in_p, conv_w=conv_w, conv_b=row(conv_b), wg=wg, gx_b=row(gx_b), ga_b=row(ga_b),
        lam=row(lru_lambda), qg=row(q_norm_g), kvg=row(kv_norm_g), wq=wq, wk=wk, wvt=wvt,
        w_out=w_out.astype(BF16), ln1_g=row(ln1_g), ln1_b=row(ln1_b),
        w_up=_interleave_halves(w_up.astype(BF16), ffn_chunk),
        fcw=_interleave_halves(ffn_conv_w * val_half, ffn_chunk),
        fcb=_interleave_halves(row(ffn_conv_b * val_half), ffn_chunk),
        w_down=w_down.astype(BF16), ln2_g=row(ln2_g), ln2_b=row(ln2_b))


def kernel(x, positions, w_in, conv_w, conv_b, gx_w, gx_b, ga_w, ga_b, lru_lambda, q_norm_g, w_uq, kv_norm_g, w_ukv, w_out, ln1_g, ln1_b, w_up, ffn_conv_w, ffn_conv_b, w_down, ln2_g, ln2_b):
    depth = w_in.shape[0]
    S = x.shape[1]
    assert S % TB == 0
    alpha = (2 * depth) ** 0.25
    tm = 2 * TB if S % (2 * TB) == 0 else TB
    rnn_chunk = 256
    ffn_chunk = 512
    p = _prepare_weights(rnn_chunk, ffn_chunk, w_in, conv_w, conv_b, gx_w, gx_b, ga_w, ga_b, lru_lambda,
                         q_norm_g, w_uq, kv_norm_g, w_ukv, w_out, ln1_g, ln1_b,
                         w_up, ffn_conv_w, ffn_conv_b, w_down, ln2_g, ln2_b)
    x = _to_strands(x)
    cos_t, sin_t = _rope_tables(_to_strands(positions), TB)
    ones_row = (np.arange(N_HEADS * V_ROWS) % V_ROWS == V_HEAD).astype(np.float32)
    vbias = jnp.asarray(np.repeat(ones_row[:, None], LANES, axis=1))
    for l in range(depth):
        pa, gb, qn, kvn, kpe = _in_proj_rglru(
            l, x, p["w_in_p"], p["conv_w"], p["conv_b"], p["wg"], p["gx_b"], p["ga_b"], p["lam"],
            p["qg"], p["kvg"], tm=tm, chunk=rnn_chunk)
        q, k, vt = _qkv_up(l, qn, kvn, kpe, cos_t, sin_t, p["wq"], p["wk"], p["wvt"], vbias)
        x = _attn_out(l, q, k, vt, pa, gb, x, p["w_out"], p["ln1_g"], p["ln1_b"], alpha=alpha)
        x = _conv_ffn(l, x, p["w_up"], p["fcw"], p["fcb"], p["w_down"], p["ln2_g"], p["ln2_b"],
                      tm=tm, chunk=ffn_chunk, alpha=alpha)
    return _from_strands(x)
```

```python
import functools
import math

import jax
import jax.numpy as jnp
import numpy as np
from jax import lax
from jax.experimental import pallas as pl
from jax.experimental.pallas import tpu as pltpu

RNN_BLOCKS = 16
CONV_WIDTH = 4
LRU_C = 8.0
N_HEADS = 16
QK_NOPE = 64
QK_ROPE = 32
V_HEAD = 64
ROPE_THETA = 10000.0
FFN_CONV_WIDTH = 3
EPS = 1e-6
NEG_INF = -1e30

LANES = 128
SUBLANES = 8
TB = 256
ROWS = TB // SUBLANES
HEAD_GROUP = LANES
ROPE_HALF = QK_ROPE // 2
HALF_GROUP = HEAD_GROUP // 2
V_ROWS = 80
VMEM_LIMIT = 56 * 1024 * 1024
SCORE_SCALE = (QK_NOPE + QK_ROPE) ** -0.5 * math.log2(math.e)

BF16 = jnp.bfloat16
F32 = jnp.float32


def _dot(a, b):
    return jnp.dot(a, b, preferred_element_type=F32)


GELU_K0 = math.sqrt(2.0 / math.pi)
GELU_K1 = 0.044715


def _gelu_tanh_arg(x, scale):
    return x * (scale * GELU_K0 + (scale ** 3 * GELU_K0 * GELU_K1) * (x * x))


def _sqrt_nonneg(x):
    return jnp.where(x > 0.0, x * lax.rsqrt(x), 0.0)


def _layer_norm(z, g, b):
    mu = jnp.mean(z, axis=-1, keepdims=True)
    zc = z - mu
    var = jnp.mean(zc * zc, axis=-1, keepdims=True)
    return zc * lax.rsqrt(var + EPS) * g + b


def _rms_norm(z, g):
    return z * lax.rsqrt(jnp.mean(z * z, axis=-1, keepdims=True) + EPS) * g


def _to_strands(a):
    B, S = a.shape[:2]
    rest = a.shape[2:]
    return a.reshape(B, S // TB, SUBLANES, ROWS, *rest).swapaxes(2, 3).reshape(B, S, *rest)


def _from_strands(a):
    B, S = a.shape[:2]
    rest = a.shape[2:]
    return a.reshape(B, S // TB, ROWS, SUBLANES, *rest).swapaxes(2, 3).reshape(B, S, *rest)


def _strand_time(r):
    return (r & (SUBLANES - 1)) * ROWS + (r >> 3)


def _delays(prev_tail, x, depth):
    n = depth * SUBLANES
    C = x.shape[1]
    sub = lax.broadcasted_iota(jnp.int32, (depth, SUBLANES, C), 1)
    own = pltpu.roll(x[TB - n:].reshape(depth, SUBLANES, C), 1, 1)
    prev = pltpu.roll(prev_tail.reshape(depth, SUBLANES, C), 1, 1)
    head = jnp.where(sub == 0, prev, own).reshape(n, C)
    ext = jnp.concatenate([head, x], axis=0)
    return [ext[(depth - k) * SUBLANES:(depth - k) * SUBLANES + TB] for k in range(1, depth + 1)]


def _blocks(z):
    return [z[i * TB:(i + 1) * TB] for i in range(z.shape[0] // TB)]


def _causal_conv(prev_ref, z, w_ref, b_ref, cs, width):
    tail = prev_ref.shape[0]
    prev = prev_ref[:, cs]
    outs = []
    for x in _blocks(z):
        out = x * w_ref[width - 1:width, cs] + b_ref[:, cs]
        for k, xd in enumerate(_delays(prev, x, width - 1), start=1):
            out = out + xd * w_ref[width - 1 - k:width - k, cs]
        outs.append(out)
        prev = x[TB - tail:]
    prev_ref[:, cs] = prev
    return jnp.concatenate(outs, axis=0)


def _layer_spec(a, l):
    nd = a.ndim - 1
    return pl.BlockSpec((None,) + a.shape[1:], lambda *_: (l,) + (0,) * nd, pipeline_mode=pl.Buffered(1))


def _tok_spec(tm, width):
    return pl.BlockSpec((None, tm, width), lambda b, s: (b, s, 0))


def _rope_tables_body(pos_ref, invf_ref, cos_ref, sin_ref):
    pos = pos_ref[...].astype(F32)
    ang = pos * invf_ref[...]
    lane = lax.broadcasted_iota(jnp.int32, ang.shape, 1)
    cosv = jnp.cos(ang)
    sinv = jnp.sin(ang)
    rot = (lane & (HALF_GROUP - 1)) < ROPE_HALF
    cos_ref[...] = jnp.where(rot, cosv, 1.0)
    sin_ref[...] = jnp.where(rot, jnp.where(lane < HALF_GROUP, -sinv, sinv), 0.0)


def _rope_tables(positions, tm):
    B, S = positions.shape
    inv_freq = ROPE_THETA ** (-jnp.arange(0, QK_ROPE, 2, dtype=F32) / QK_ROPE)
    idx = np.minimum(np.arange(LANES) % HALF_GROUP, ROPE_HALF - 1)
    invf = inv_freq[idx].reshape(1, LANES)
    pos = positions.reshape(B, S, 1)
    out = jax.ShapeDtypeStruct((B, S, LANES), F32)
    return pl.pallas_call(
        _rope_tables_body,
        grid=(B, S // tm),
        in_specs=[_tok_spec(tm, 1), pl.BlockSpec((1, LANES), lambda b, s: (0, 0))],
        out_specs=[_tok_spec(tm, LANES), _tok_spec(tm, LANES)],
        out_shape=[out, out],
        compiler_params=pltpu.CompilerParams(dimension_semantics=("parallel", "parallel")),
        name="rope_tables",
    )(pos, invf)


def _lru_scan(a, u, h0):
    C = a.shape[1]
    row = lambda z, v: z[v * SUBLANES:(v + 1) * SUBLANES]
    p, h = row(a, 0), row(u, 0)
    ps, hs = [p], [h]
    for v in range(1, ROWS):
        av = row(a, v)
        p = av * p
        h = av * h + row(u, v)
        ps.append(p)
        hs.append(h)
    sub = lax.broadcasted_iota(jnp.int32, (SUBLANES, C), 0)
    pc, hc = p, h
    step = 1
    while step < SUBLANES:
        m = sub >= step
        rp = pltpu.roll(pc, step, 0)
        rh = pltpu.roll(hc, step, 0)
        hc = jnp.where(m, pc * rh + hc, hc)
        pc = jnp.where(m, pc * rp, pc)
        step *= 2
    end = hc + pc * h0
    init = jnp.where(sub == 0, h0, pltpu.roll(end, 1, 0))
    out = jnp.concatenate([hs[v] + ps[v] * init for v in range(ROWS)], axis=0)
    return out, end[SUBLANES - 1:]


def _in_proj_body(d_rnn, d_mix, q_lora, kv_lora, chunk,
                  x_ref, w_ref, cw_ref, cb_ref, wg_ref, gxb_ref, gab_ref, lam_ref, qg_ref, kvg_ref,
                  pa_ref, gb_ref, qn_ref, kvn_ref, kpe_ref, xprev_ref, hc_ref):
    @pl.when(pl.program_id(1) == 0)
    def _():
        xprev_ref[...] = jnp.zeros_like(xprev_ref)
        hc_ref[...] = jnp.zeros_like(hc_ref)

    xb = x_ref[...].astype(BF16)
    neg_lam = -lam_ref[...]
    softplus = jnp.maximum(neg_lam, 0.0) + jnp.log1p(jnp.exp(-jnp.abs(neg_lam)))
    half_rate = (-0.5 * LRU_C) * softplus
    off_g, off_a, off_b = d_rnn, 2 * d_rnn, 2 * d_rnn + d_mix
    off_q = 2 * d_rnn + 2 * d_mix
    off_kv = off_q + q_lora
    off_kpe = off_kv + kv_lora

    for c in range(d_rnn // chunk):
        cs = slice(c * chunk, (c + 1) * chunk)
        xr = _dot(xb, w_ref[:, cs])
        conv = _causal_conv(xprev_ref, xr, cw_ref, cb_ref, cs, CONV_WIDTH)

        gates = _dot(conv.astype(BF16), wg_ref[c])
        tx = jnp.tanh(gates[:, :chunk] + gxb_ref[:, cs])
        tr = jnp.tanh(gates[:, chunk:] + gab_ref[:, cs])
        log_a = half_rate[:, cs] * tr + half_rate[:, cs]
        a = jnp.exp(log_a)
        t = jnp.tanh(log_a)
        half_mult = _sqrt_nonneg((-0.5 * t) / (1.0 - t))
        u = (half_mult * conv) * (tx + 1.0)
        hs, hlast = [], hc_ref[:, cs]
        for a_blk, u_blk in zip(_blocks(a), _blocks(u)):
            h_blk, hlast = _lru_scan(a_blk, u_blk, hlast)
            hs.append(h_blk)
        h = jnp.concatenate(hs, axis=0)
        hc_ref[:, cs] = hlast

        gq = _dot(xb, w_ref[:, off_g + c * chunk:off_g + (c + 1) * chunk])
        tg = jnp.tanh(_gelu_tanh_arg(gq, 4.0))
        ta = jnp.tanh(_dot(xb, w_ref[:, off_a + c * chunk:off_a + (c + 1) * chunk]))
        pa_ref[:, cs] = ((1.0 + tg) * (1.0 + ta)) * (gq * h)
        gb_ref[:, cs] = _dot(xb, w_ref[:, off_b + c * chunk:off_b + (c + 1) * chunk])

    q_lat = _dot(xb, w_ref[:, off_q:off_kv])
    qn_ref[...] = _rms_norm(q_lat, qg_ref[...]).astype(BF16)
    kv_lat = _dot(xb, w_ref[:, off_kv:off_kpe])
    kvn_ref[...] = _rms_norm(kv_lat, kvg_ref[...]).astype(BF16)
    kpe_ref[...] = _dot(xb, w_ref[:, off_kpe:off_kpe + HEAD_GROUP])


def _in_proj_rglru(l, x, w_in_p, conv_w, conv_b, wg, gx_b, ga_b, lam, qg, kvg, *, tm, chunk):
    B, S, D = x.shape
    d_rnn = conv_w.shape[2]
    q_lora, kv_lora = qg.shape[2], kvg.shape[2]
    d_mix = (w_in_p.shape[2] - 2 * d_rnn - q_lora - kv_lora - HEAD_GROUP) // 2
    body = functools.partial(_in_proj_body, d_rnn, d_mix, q_lora, kv_lora, chunk)
    ins = [x, w_in_p, conv_w, conv_b, wg, gx_b, ga_b, lam, qg, kvg]
    in_specs = [_tok_spec(tm, D)] + [_layer_spec(a, l) for a in ins[1:]]
    outs = [(d_mix, F32), (d_mix, F32), (q_lora, BF16), (kv_lora, BF16), (HEAD_GROUP, F32)]
    return pl.pallas_call(
        body,
        grid=(B, S // tm),
        in_specs=in_specs,
        out_specs=[_tok_spec(tm, w) for w, _ in outs],
        out_shape=[jax.ShapeDtypeStruct((B, S, w), dt) for w, dt in outs],
        scratch_shapes=[pltpu.VMEM(((CONV_WIDTH - 1) * SUBLANES, d_rnn), F32), pltpu.VMEM((1, d_rnn), F32)],
        compiler_params=pltpu.CompilerParams(
            dimension_semantics=("arbitrary", "arbitrary"), vmem_limit_bytes=VMEM_LIMIT),
        name="in_proj_rglru",
    )(*ins)


def _qkv_body(qn_ref, kvn_ref, kpe_ref, cos_ref, sin_ref, wq_ref, wk_ref, wvt_ref, vbias_ref, q_ref, k_ref, vt_ref):
    cosv = cos_ref[...]
    sinv = sin_ref[...]

    def rope(z):
        return z * cosv + pltpu.roll(z, HALF_GROUP, 1) * sinv

    kpe = rope(kpe_ref[...])
    kvn = kvn_ref[...]
    qf = _dot(qn_ref[...], wq_ref[...])
    kf = _dot(kvn, wk_ref[...])
    for h in range(qf.shape[1] // HEAD_GROUP):
        sl = slice(h * HEAD_GROUP, (h + 1) * HEAD_GROUP)
        q_ref[:, sl] = (rope(qf[:, sl]) * SCORE_SCALE).astype(BF16)
        k_ref[:, sl] = (kf[:, sl] + kpe).astype(BF16)
    vt = lax.dot_general(wvt_ref[...], kvn, (((1,), (1,)), ((), ())), preferred_element_type=F32)
    vbias = vbias_ref[...]
    for c in range(vt.shape[1] // LANES):
        sl = slice(c * LANES, (c + 1) * LANES)
        vt_ref[:, sl] = (vt[:, sl] + vbias).astype(BF16)


def _qkv_up(l, qn, kvn, kpe, cos_t, sin_t, wq, wk, wvt, vbias):
    B, S, _ = qn.shape
    ins = [qn, kvn, kpe, cos_t, sin_t, wq, wk, wvt, vbias]
    in_specs = ([_tok_spec(TB, a.shape[2]) for a in ins[:5]] + [_layer_spec(a, l) for a in ins[5:8]]
                + [pl.BlockSpec(vbias.shape, lambda b, s: (0, 0), pipeline_mode=pl.Buffered(1))])
    rows = wvt.shape[1]
    return pl.pallas_call(
        _qkv_body,
        grid=(B, S // TB),
        in_specs=in_specs,
        out_specs=[_tok_spec(TB, wq.shape[2]), _tok_spec(TB, wk.shape[2]),
                   pl.BlockSpec((None, None, rows, TB), lambda b, s: (b, s, 0, 0))],
        out_shape=[jax.ShapeDtypeStruct((B, S, wq.shape[2]), BF16),
                   jax.ShapeDtypeStruct((B, S, wk.shape[2]), BF16),
                   jax.ShapeDtypeStruct((B, S // TB, rows, TB), BF16)],
        compiler_params=pltpu.CompilerParams(
            dimension_semantics=("parallel", "parallel"), vmem_limit_bytes=VMEM_LIMIT),
        name="qkv_up",
    )(*ins)


def _attn_body(alpha, q_ref, k_ref, vt_ref, qn_ref, kn_ref, pa_ref, gb_ref, x_ref, wo_ref, g_ref, b_ref, o_ref,
               s_ref, m_ref, acc_ref):
    n_heads = m_ref.shape[0]
    qi = pl.program_id(1)
    key_t = _strand_time(lax.broadcasted_iota(jnp.int32, (TB, TB), 0))
    qry_t = _strand_time(lax.broadcasted_iota(jnp.int32, (TB, TB), 1))
    mask_bias = jnp.where(key_t <= qry_t, 0.0, NEG_INF)
    m_ref[...] = jnp.full(m_ref.shape, NEG_INF, F32)
    acc_ref[...] = jnp.zeros(acc_ref.shape, F32)
    head = lambda h: slice(h * HEAD_GROUP, (h + 1) * HEAD_GROUP)

    def scores(k_blk, q_blk):
        return lax.dot_general(k_blk, q_blk, (((1,), (1,)), ((), ())), preferred_element_type=F32)

    def attend(j, h, s):
        m_old = m_ref[h]
        m_new = jnp.maximum(m_old, jnp.max(s, axis=0, keepdims=True))
        p = jnp.exp2(s - m_new).astype(BF16)
        pv = _dot(vt_ref[j, h * V_ROWS:(h + 1) * V_ROWS, :], p)
        acc_ref[h] = acc_ref[h] * jnp.exp2(m_old - m_new) + pv
        m_ref[h] = m_new

    @pl.when((pl.program_id(0) == 0) & (qi == 0))
    def _():
        for h in range(n_heads):
            s_ref[h] = scores(k_ref[0:TB, head(h)], q_ref[:, head(h)])

    def body(j, carry):
        off = pl.multiple_of((j + 1) * TB, TB)
        for h in range(n_heads):
            s = s_ref[h]
            s_ref[h] = scores(k_ref[pl.ds(off, TB), head(h)], q_ref[:, head(h)])
            attend(j, h, s)
        return carry

    lax.fori_loop(0, qi, body, 0)
    for h in range(n_heads):
        s = s_ref[h]
        s_ref[h] = scores(kn_ref[:, head(h)], qn_ref[:, head(h)])
        attend(qi, h, s + mask_bias)

    ys = []
    for h in range(n_heads):
        a = acc_ref[h]
        ys.append(a[:V_HEAD] / a[V_HEAD:V_HEAD + 1])
    y = jnp.concatenate(ys, axis=0).T
    merged = pa_ref[...] + (0.5 * jnp.tanh(gb_ref[...]) + 0.5) * y
    o = _dot(merged.astype(BF16), wo_ref[...])
    o_ref[...] = _layer_norm(alpha * x_ref[...] + o, g_ref[...], b_ref[...])


def _attn_out(l, q, k, vt, pa, gb, x, w_out, g, b, *, alpha):
    B, S, D = x.shape
    n_heads = k.shape[2] // HEAD_GROUP
    n_steps = S // TB
    assert vt.shape[3] == TB and vt.shape[2] == n_heads * V_ROWS

    def next_step(bi, s):
        wrap = s + 1 == n_steps
        return jnp.where(wrap, jnp.minimum(bi + 1, B - 1), bi), jnp.where(wrap, 0, s + 1)

    in_specs = [_tok_spec(TB, q.shape[2]),
                pl.BlockSpec((None, S, k.shape[2]), lambda bi, s: (bi, 0, 0)),
                pl.BlockSpec((None,) + vt.shape[1:], lambda bi, s: (bi, 0, 0, 0)),
                pl.BlockSpec((None, TB, q.shape[2]), lambda bi, s: (*next_step(bi, s), 0)),
                pl.BlockSpec((None, TB, k.shape[2]), lambda bi, s: (next_step(bi, s)[0], 0, 0)),
                _tok_spec(TB, pa.shape[2]), _tok_spec(TB, gb.shape[2]), _tok_spec(TB, D),
                _layer_spec(w_out, l), _layer_spec(g, l), _layer_spec(b, l)]
    return pl.pallas_call(
        functools.partial(_attn_body, alpha),
        grid=(B, n_steps),
        in_specs=in_specs,
        out_specs=_tok_spec(TB, D),
        out_shape=jax.ShapeDtypeStruct((B, S, D), F32),
        scratch_shapes=[pltpu.VMEM((n_heads, TB, TB), F32), pltpu.VMEM((n_heads, 1, TB), F32),
                        pltpu.VMEM((n_heads, V_ROWS, TB), F32)],
        compiler_params=pltpu.CompilerParams(
            dimension_semantics=("arbitrary", "arbitrary"), vmem_limit_bytes=VMEM_LIMIT),
        name="attn_out",
    )(q, k, vt, q, k, pa, gb, x, w_out, g, b)


def _ffn_body(alpha, x_ref, wu_ref, cw_ref, cb_ref, wd_ref, g_ref, b_ref, o_ref, hprev_ref):
    @pl.when(pl.program_id(1) == 0)
    def _():
        hprev_ref[...] = jnp.zeros_like(hprev_ref)

    d_ff = wd_ref.shape[0]
    x = x_ref[...]
    hconv = _causal_conv(hprev_ref, _dot(x.astype(BF16), wu_ref[...]), cw_ref, cb_ref, slice(None), FFN_CONV_WIDTH)
    hg, half_hv = hconv[:, :d_ff], hconv[:, d_ff:]
    act = ((hg * half_hv) * (1.0 + jnp.tanh(_gelu_tanh_arg(hg, 1.0)))).astype(BF16)
    o_ref[...] = _layer_norm(alpha * x + _dot(act, wd_ref[...]), g_ref[...], b_ref[...])


def _conv_ffn(l, x, w_up, cw, cb, w_down, g, b, *, tm, alpha):
    B, S, D = x.shape
    ins = [x, w_up, cw, cb, w_down, g, b]
    return pl.pallas_call(
        functools.partial(_ffn_body, alpha),
        grid=(B, S // tm),
        in_specs=[_tok_spec(tm, D)] + [_layer_spec(a, l) for a in ins[1:]],
        out_specs=_tok_spec(tm, D),
        out_shape=jax.ShapeDtypeStruct((B, S, D), F32),
        scratch_shapes=[pltpu.VMEM(((FFN_CONV_WIDTH - 1) * SUBLANES, w_up.shape[2]), F32)],
        compiler_params=pltpu.CompilerParams(
            dimension_semantics=("arbitrary", "arbitrary"), vmem_limit_bytes=VMEM_LIMIT),
        name="conv_ffn",
    )(*ins)


def _block_diag(w):
    L, H, r, _ = w.shape
    eye = jnp.eye(H, dtype=w.dtype)
    return (eye[:, None, :, None] * w[:, :, :, None, :]).reshape(L, H * r, H * r)


def _head_group(nope, rope):
    ref = nope if nope is not None else rope
    z = lambda n: jnp.zeros(ref.shape[:-1] + (n,), ref.dtype)
    r0, r1 = (z(ROPE_HALF), z(ROPE_HALF)) if rope is None else (rope[..., :ROPE_HALF], rope[..., ROPE_HALF:])
    split = HALF_GROUP - ROPE_HALF
    n0, n1 = (z(split), z(QK_NOPE - split)) if nope is None else (nope[..., :split], nope[..., split:])
    return jnp.concatenate([r0, n0, r1, n1, z(HEAD_GROUP - QK_NOPE - QK_ROPE)], axis=-1)


def _prepare_weights(rnn_chunk, w_in, conv_w, conv_b, gx_w, gx_b, ga_w, ga_b, lru_lambda,
                     q_norm_g, w_uq, kv_norm_g, w_ukv, w_out, ln1_g, ln1_b,
                     w_up, ffn_conv_w, ffn_conv_b, w_down, ln2_g, ln2_b):
    L = w_in.shape[0]
    d_rnn = conv_w.shape[2]
    q_lora, kv_lora = w_uq.shape[1], w_ukv.shape[1]
    d_ff = w_down.shape[1]
    p_q = 2 * d_rnn
    p_kv = p_q + q_lora
    p_kr = p_kv + kv_lora
    p_a = p_kr + QK_ROPE
    w_in_p = jnp.concatenate(
        [w_in[..., :d_rnn], 0.25 * w_in[..., d_rnn:p_q], 0.5 * w_in[..., p_a:], w_in[..., p_q:p_kr],
         _head_group(None, w_in[..., p_kr:p_a])], axis=-1).astype(BF16)
    gx_bd, ga_bd = _block_diag(0.5 * gx_w), _block_diag(0.5 * ga_w)
    ch = rnn_chunk
    wg = jnp.stack([
        jnp.concatenate([gx_bd[:, c * ch:(c + 1) * ch, c * ch:(c + 1) * ch],
                         ga_bd[:, c * ch:(c + 1) * ch, c * ch:(c + 1) * ch]], axis=-1)
        for c in range(d_rnn // ch)], axis=1).astype(BF16)
    row = lambda a: a[:, None, :]
    wq = w_uq.reshape(L, q_lora, N_HEADS, QK_NOPE + QK_ROPE)
    wq = _head_group(wq[..., :QK_NOPE], wq[..., QK_NOPE:]).reshape(L, q_lora, N_HEADS * HEAD_GROUP).astype(BF16)
    wkv = w_ukv.reshape(L, kv_lora, N_HEADS, QK_NOPE + V_HEAD)
    wk = _head_group(wkv[..., :QK_NOPE], None).reshape(L, kv_lora, N_HEADS * HEAD_GROUP).astype(BF16)
    wvt = jnp.pad(jnp.transpose(wkv[..., QK_NOPE:], (0, 2, 3, 1)), ((0, 0), (0, 0), (0, V_ROWS - V_HEAD), (0, 0)))
    wvt = wvt.reshape(L, N_HEADS * V_ROWS, kv_lora).astype(BF16)
    val_half = jnp.concatenate([jnp.ones((d_ff,), F32), jnp.full((d_ff,), 0.5, F32)])
    return dict(
        w_in_p=w_in_p, conv_w=conv_w, conv_b=row(conv_b), wg=wg, gx_b=row(0.5 * gx_b), ga_b=row(0.5 * ga_b),
        lam=row(lru_lambda), qg=row(q_norm_g), kvg=row(kv_norm_g), wq=wq, wk=wk, wvt=wvt,
        w_out=w_out.astype(BF16), ln1_g=row(ln1_g), ln1_b=row(ln1_b),
        w_up=w_up.astype(BF16), fcw=ffn_conv_w * val_half, fcb=row(ffn_conv_b * val_half),
        w_down=w_down.astype(BF16), ln2_g=row(ln2_g), ln2_b=row(ln2_b))


def kernel(x, positions, w_in, conv_w, conv_b, gx_w, gx_b, ga_w, ga_b, lru_lambda, q_norm_g, w_uq, kv_norm_g, w_ukv, w_out, ln1_g, ln1_b, w_up, ffn_conv_w, ffn_conv_b, w_down, ln2_g, ln2_b):
    depth = w_in.shape[0]
    S = x.shape[1]
    assert S % TB == 0
    alpha = (2 * depth) ** 0.25
    tm = 2 * TB if S % (2 * TB) == 0 else TB
    rnn_chunk = 256
    p = _prepare_weights(rnn_chunk, w_in, conv_w, conv_b, gx_w, gx_b, ga_w, ga_b, lru_lambda,
                         q_norm_g, w_uq, kv_norm_g, w_ukv, w_out, ln1_g, ln1_b,
                         w_up, ffn_conv_w, ffn_conv_b, w_down, ln2_g, ln2_b)
    x = _to_strands(x)
    cos_t, sin_t = _rope_tables(_to_strands(positions), TB)
    ones_row = (np.arange(N_HEADS * V_ROWS) % V_ROWS == V_HEAD).astype(np.float32)
    vbias = jnp.asarray(np.repeat(ones_row[:, None], LANES, axis=1))
    for l in range(depth):
        pa, gb, qn, kvn, kpe = _in_proj_rglru(
            l, x, p["w_in_p"], p["conv_w"], p["conv_b"], p["wg"], p["gx_b"], p["ga_b"], p["lam"],
            p["qg"], p["kvg"], tm=tm, chunk=rnn_chunk)
        q, k, vt = _qkv_up(l, qn, kvn, kpe, cos_t, sin_t, p["wq"], p["wk"], p["wvt"], vbias)
        x = _attn_out(l, q, k, vt, pa, gb, x, p["w_out"], p["ln1_g"], p["ln1_b"], alpha=alpha)
        x = _conv_ffn(l, x, p["w_up"], p["fcw"], p["fcb"], p["w_down"], p["ln2_g"], p["ln2_b"],
                      tm=tm, alpha=alpha)
    return _from_strands(x)
```

```python
import functools
import math

import jax
import jax.numpy as jnp
import numpy as np
from jax import lax
from jax.experimental import pallas as pl
from jax.experimental.pallas import tpu as pltpu

RNN_BLOCKS = 16
CONV_WIDTH = 4
LRU_C = 8.0
N_HEADS = 16
QK_NOPE = 64
QK_ROPE = 32
V_HEAD = 64
ROPE_THETA = 10000.0
FFN_CONV_WIDTH = 3
EPS = 1e-6
NEG_INF = -1e30

LANES = 128
SUBLANES = 8
TB = 256
ROWS = TB // SUBLANES
HEAD_GROUP = LANES
ROPE_HALF = QK_ROPE // 2
HALF_GROUP = HEAD_GROUP // 2
V_ROWS = 80
VMEM_LIMIT = 56 * 1024 * 1024
SCORE_SCALE = (QK_NOPE + QK_ROPE) ** -0.5 * math.log2(math.e)

BF16 = jnp.bfloat16
F32 = jnp.float32


def _dot(a, b):
    return jnp.dot(a, b, preferred_element_type=F32)


GELU_K0 = math.sqrt(2.0 / math.pi)
GELU_K1 = 0.044715


def _gelu_tanh_arg(x, scale):
    return x * (scale * GELU_K0 + (scale ** 3 * GELU_K0 * GELU_K1) * (x * x))


def _sqrt_nonneg(x):
    return jnp.where(x > 0.0, x * lax.rsqrt(x), 0.0)


def _layer_norm(z, g, b):
    mu = jnp.mean(z, axis=-1, keepdims=True)
    zc = z - mu
    var = jnp.mean(zc * zc, axis=-1, keepdims=True)
    return zc * lax.rsqrt(var + EPS) * g + b


def _rms_norm(z, g):
    return z * lax.rsqrt(jnp.mean(z * z, axis=-1, keepdims=True) + EPS) * g


def _to_strands(a):
    B, S = a.shape[:2]
    rest = a.shape[2:]
    return a.reshape(B, S // TB, SUBLANES, ROWS, *rest).swapaxes(2, 3).reshape(B, S, *rest)


def _from_strands(a):
    B, S = a.shape[:2]
    rest = a.shape[2:]
    return a.reshape(B, S // TB, ROWS, SUBLANES, *rest).swapaxes(2, 3).reshape(B, S, *rest)


def _strand_time(r):
    return (r & (SUBLANES - 1)) * ROWS + (r >> 3)


def _delays(prev_tail, x, depth):
    n = depth * SUBLANES
    C = x.shape[1]
    sub = lax.broadcasted_iota(jnp.int32, (depth, SUBLANES, C), 1)
    own = pltpu.roll(x[TB - n:].reshape(depth, SUBLANES, C), 1, 1)
    prev = pltpu.roll(prev_tail.reshape(depth, SUBLANES, C), 1, 1)
    head = jnp.where(sub == 0, prev, own).reshape(n, C)
    ext = jnp.concatenate([head, x], axis=0)
    return [ext[(depth - k) * SUBLANES:(depth - k) * SUBLANES + TB] for k in range(1, depth + 1)]


def _blocks(z):
    return [z[i * TB:(i + 1) * TB] for i in range(z.shape[0] // TB)]


def _causal_conv(prev_ref, z, w_ref, b_ref, cs, width):
    tail = prev_ref.shape[0]
    prev = prev_ref[:, cs]
    outs = []
    for x in _blocks(z):
        out = x * w_ref[width - 1:width, cs] + b_ref[:, cs]
        for k, xd in enumerate(_delays(prev, x, width - 1), start=1):
            out = out + xd * w_ref[width - 1 - k:width - k, cs]
        outs.append(out)
        prev = x[TB - tail:]
    prev_ref[:, cs] = prev
    return jnp.concatenate(outs, axis=0)


def _layer_spec(a, l):
    nd = a.ndim - 1
    return pl.BlockSpec((None,) + a.shape[1:], lambda *_: (l,) + (0,) * nd, pipeline_mode=pl.Buffered(1))


def _tok_spec(tm, width):
    return pl.BlockSpec((None, tm, width), lambda b, s: (b, s, 0))


def _rope_tables_body(pos_ref, invf_ref, cos_ref, sin_ref):
    pos = pos_ref[...].astype(F32)
    ang = pos * invf_ref[...]
    lane = lax.broadcasted_iota(jnp.int32, ang.shape, 1)
    cosv = jnp.cos(ang)
    sinv = jnp.sin(ang)
    rot = (lane & (HALF_GROUP - 1)) < ROPE_HALF
    cos_ref[...] = jnp.where(rot, cosv, 1.0)
    sin_ref[...] = jnp.where(rot, jnp.where(lane < HALF_GROUP, -sinv, sinv), 0.0)


def _rope_tables(positions, tm):
    B, S = positions.shape
    inv_freq = ROPE_THETA ** (-jnp.arange(0, QK_ROPE, 2, dtype=F32) / QK_ROPE)
    idx = np.minimum(np.arange(LANES) % HALF_GROUP, ROPE_HALF - 1)
    invf = inv_freq[idx].reshape(1, LANES)
    pos = positions.reshape(B, S, 1)
    out = jax.ShapeDtypeStruct((B, S, LANES), F32)
    return pl.pallas_call(
        _rope_tables_body,
        grid=(B, S // tm),
        in_specs=[_tok_spec(tm, 1), pl.BlockSpec((1, LANES), lambda b, s: (0, 0))],
        out_specs=[_tok_spec(tm, LANES), _tok_spec(tm, LANES)],
        out_shape=[out, out],
        compiler_params=pltpu.CompilerParams(dimension_semantics=("parallel", "parallel")),
        name="rope_tables",
    )(pos, invf)


def _lru_scan(a, u, h0):
    C = a.shape[1]
    row = lambda z, v: z[v * SUBLANES:(v + 1) * SUBLANES]
    p, h = row(a, 0), row(u, 0)
    ps, hs = [p], [h]
    for v in range(1, ROWS):
        av = row(a, v)
        p = av * p
        h = av * h + row(u, v)
        ps.append(p)
        hs.append(h)
    sub = lax.broadcasted_iota(jnp.int32, (SUBLANES, C), 0)
    pc, hc = p, h
    step = 1
    while step < SUBLANES:
        m = sub >= step
        rp = pltpu.roll(pc, step, 0)
        rh = pltpu.roll(hc, step, 0)
        hc = jnp.where(m, pc * rh + hc, hc)
        pc = jnp.where(m, pc * rp, pc)
        step *= 2
    end = hc + pc * h0
    init = jnp.where(sub == 0, h0, pltpu.roll(end, 1, 0))
    out = jnp.concatenate([hs[v] + ps[v] * init for v in range(ROWS)], axis=0)
    return out, end[SUBLANES - 1:]


def _in_proj_body(d_rnn, d_mix, q_lora, kv_lora, chunk,
                  x_ref, cos_ref, sin_ref, w_ref, cw_ref, cb_ref, wg_ref, gxb_ref, gab_ref, lam_ref, qg_ref, kvg_ref,
                  wq_ref, wk_ref, wvt_ref, vbias_ref,
                  pa_ref, gb_ref, q_ref, k_ref, vt_ref, xprev_ref, hc_ref):
    @pl.when(pl.program_id(1) == 0)
    def _():
        xprev_ref[...] = jnp.zeros_like(xprev_ref)
        hc_ref[...] = jnp.zeros_like(hc_ref)

    xb = x_ref[...].astype(BF16)
    neg_lam = -lam_ref[...]
    softplus = jnp.maximum(neg_lam, 0.0) + jnp.log1p(jnp.exp(-jnp.abs(neg_lam)))
    half_rate = (-0.5 * LRU_C) * softplus
    off_g, off_a, off_b = d_rnn, 2 * d_rnn, 2 * d_rnn + d_mix
    off_q = 2 * d_rnn + 2 * d_mix
    off_kv = off_q + q_lora
    off_kpe = off_kv + kv_lora

    for c in range(d_rnn // chunk):
        cs = slice(c * chunk, (c + 1) * chunk)
        xr = _dot(xb, w_ref[:, cs])
        conv = _causal_conv(xprev_ref, xr, cw_ref, cb_ref, cs, CONV_WIDTH)

        gates = _dot(conv.astype(BF16), wg_ref[c])
        tx = jnp.tanh(gates[:, :chunk] + gxb_ref[:, cs])
        tr = jnp.tanh(gates[:, chunk:] + gab_ref[:, cs])
        log_a = half_rate[:, cs] * tr + half_rate[:, cs]
        a = jnp.exp(log_a)
        t = jnp.tanh(log_a)
        half_mult = _sqrt_nonneg((-0.5 * t) / (1.0 - t))
        u = (half_mult * conv) * (tx + 1.0)
        hs, hlast = [], hc_ref[:, cs]
        for a_blk, u_blk in zip(_blocks(a), _blocks(u)):
            h_blk, hlast = _lru_scan(a_blk, u_blk, hlast)
            hs.append(h_blk)
        h = jnp.concatenate(hs, axis=0)
        hc_ref[:, cs] = hlast

        gq = _dot(xb, w_ref[:, off_g + c * chunk:off_g + (c + 1) * chunk])
        tg = jnp.tanh(_gelu_tanh_arg(gq, 4.0))
        ta = jnp.tanh(_dot(xb, w_ref[:, off_a + c * chunk:off_a + (c + 1) * chunk]))
        pa_ref[:, cs] = ((1.0 + tg) * (1.0 + ta)) * (gq * h)
        gb_ref[:, cs] = _dot(xb, w_ref[:, off_b + c * chunk:off_b + (c + 1) * chunk])

    cosv = cos_ref[...]
    sinv = sin_ref[...]

    def rope(z):
        return z * cosv + pltpu.roll(z, HALF_GROUP, 1) * sinv

    qn = _rms_norm(_dot(xb, w_ref[:, off_q:off_kv]), qg_ref[...]).astype(BF16)
    kvn = _rms_norm(_dot(xb, w_ref[:, off_kv:off_kpe]), kvg_ref[...]).astype(BF16)
    kpe = rope(_dot(xb, w_ref[:, off_kpe:off_kpe + HEAD_GROUP]))
    qf = _dot(qn, wq_ref[...])
    kf = _dot(kvn, wk_ref[...])
    for h in range(qf.shape[1] // HEAD_GROUP):
        sl = slice(h * HEAD_GROUP, (h + 1) * HEAD_GROUP)
        q_ref[:, sl] = (rope(qf[:, sl]) * SCORE_SCALE).astype(BF16)
        k_ref[:, sl] = (kf[:, sl] + kpe).astype(BF16)
    vbias = vbias_ref[...]
    for i, kvn_blk in enumerate(_blocks(kvn)):
        vt = lax.dot_general(wvt_ref[...], kvn_blk, (((1,), (1,)), ((), ())), preferred_element_type=F32)
        for c in range(TB // LANES):
            sl = slice(c * LANES, (c + 1) * LANES)
            vt_ref[i, :, sl] = (vt[:, sl] + vbias).astype(BF16)


def _in_proj_rglru(l, x, cos_t, sin_t, w_in_p, conv_w, conv_b, wg, gx_b, ga_b, lam, qg, kvg, wq, wk, wvt, vbias,
                   *, tm, chunk):
    B, S, D = x.shape
    d_rnn = conv_w.shape[2]
    q_lora, kv_lora = qg.shape[2], kvg.shape[2]
    d_mix = (w_in_p.shape[2] - 2 * d_rnn - q_lora - kv_lora - HEAD_GROUP) // 2
    body = functools.partial(_in_proj_body, d_rnn, d_mix, q_lora, kv_lora, chunk)
    layer_params = [w_in_p, conv_w, conv_b, wg, gx_b, ga_b, lam, qg, kvg, wq, wk, wvt]
    in_specs = ([_tok_spec(tm, D), _tok_spec(tm, LANES), _tok_spec(tm, LANES)]
                + [_layer_spec(a, l) for a in layer_params]
                + [pl.BlockSpec(vbias.shape, lambda b, s: (0, 0), pipeline_mode=pl.Buffered(1))])
    rows = wvt.shape[1]
    return pl.pallas_call(
        body,
        grid=(B, S // tm),
        in_specs=in_specs,
        out_specs=[_tok_spec(tm, d_mix), _tok_spec(tm, d_mix), _tok_spec(tm, wq.shape[2]), _tok_spec(tm, wk.shape[2]),
                   pl.BlockSpec((None, tm // TB, rows, TB), lambda b, s: (b, s, 0, 0))],
        out_shape=[jax.ShapeDtypeStruct((B, S, d_mix), F32), jax.ShapeDtypeStruct((B, S, d_mix), F32),
                   jax.ShapeDtypeStruct((B, S, wq.shape[2]), BF16), jax.ShapeDtypeStruct((B, S, wk.shape[2]), BF16),
                   jax.ShapeDtypeStruct((B, S // TB, rows, TB), BF16)],
        scratch_shapes=[pltpu.VMEM(((CONV_WIDTH - 1) * SUBLANES, d_rnn), F32), pltpu.VMEM((1, d_rnn), F32)],
        compiler_params=pltpu.CompilerParams(
            dimension_semantics=("arbitrary", "arbitrary"), vmem_limit_bytes=VMEM_LIMIT),
        name="in_proj_rglru",
    )(x, cos_t, sin_t, *layer_params, vbias)


def _attn_body(alpha, q_ref, k_ref, vt_ref, qn_ref, kn_ref, pa_ref, gb_ref, x_ref, wo_ref, g_ref, b_ref, o_ref,
               s_ref, m_ref, acc_ref):
    n_heads = m_ref.shape[0]
    qi = pl.program_id(1)
    key_t = _strand_time(lax.broadcasted_iota(jnp.int32, (TB, TB), 0))
    qry_t = _strand_time(lax.broadcasted_iota(jnp.int32, (TB, TB), 1))
    mask_bias = jnp.where(key_t <= qry_t, 0.0, NEG_INF)
    m_ref[...] = jnp.full(m_ref.shape, NEG_INF, F32)
    acc_ref[...] = jnp.zeros(acc_ref.shape, F32)
    head = lambda h: slice(h * HEAD_GROUP, (h + 1) * HEAD_GROUP)

    def scores(k_blk, q_blk):
        return lax.dot_general(k_blk, q_blk, (((1,), (1,)), ((), ())), preferred_element_type=F32)

    def attend(j, h, s):
        m_old = m_ref[h]
        m_new = jnp.maximum(m_old, jnp.max(s, axis=0, keepdims=True))
        p = jnp.exp2(s - m_new).astype(BF16)
        pv = _dot(vt_ref[j, h * V_ROWS:(h + 1) * V_ROWS, :], p)
        acc_ref[h] = acc_ref[h] * jnp.exp2(m_old - m_new) + pv
        m_ref[h] = m_new

    @pl.when((pl.program_id(0) == 0) & (qi == 0))
    def _():
        for h in range(n_heads):
            s_ref[h] = scores(k_ref[0:TB, head(h)], q_ref[:, head(h)])

    def body(j, carry):
        off = pl.multiple_of((j + 1) * TB, TB)
        for h in range(n_heads):
            s = s_ref[h]
            s_ref[h] = scores(k_ref[pl.ds(off, TB), head(h)], q_ref[:, head(h)])
            attend(j, h, s)
        return carry

    lax.fori_loop(0, qi, body, 0)
    for h in range(n_heads):
        s = s_ref[h]
        s_ref[h] = scores(kn_ref[:, head(h)], qn_ref[:, head(h)])
        attend(qi, h, s + mask_bias)

    ys = []
    for h in range(n_heads):
        a = acc_ref[h]
        ys.append(a[:V_HEAD] / a[V_HEAD:V_HEAD + 1])
    y = jnp.concatenate(ys, axis=0).T
    merged = pa_ref[...] + (0.5 * jnp.tanh(gb_ref[...]) + 0.5) * y
    o = _dot(merged.astype(BF16), wo_ref[...])
    o_ref[...] = _layer_norm(alpha * x_ref[...] + o, g_ref[...], b_ref[...])


def _attn_out(l, q, k, vt, pa, gb, x, w_out, g, b, *, alpha):
    B, S, D = x.shape
    n_heads = k.shape[2] // HEAD_GROUP
    n_steps = S // TB
    assert vt.shape[3] == TB and vt.shape[2] == n_heads * V_ROWS

    def next_step(bi, s):
        wrap = s + 1 == n_steps
        return jnp.where(wrap, jnp.minimum(bi + 1, B - 1), bi), jnp.where(wrap, 0, s + 1)

    in_specs = [_tok_spec(TB, q.shape[2]),
                pl.BlockSpec((None, S, k.shape[2]), lambda bi, s: (bi, 0, 0)),
                pl.BlockSpec((None,) + vt.shape[1:], lambda bi, s: (bi, 0, 0, 0)),
                pl.BlockSpec((None, TB, q.shape[2]), lambda bi, s: (*next_step(bi, s), 0)),
                pl.BlockSpec((None, TB, k.shape[2]), lambda bi, s: (next_step(bi, s)[0], 0, 0)),
                _tok_spec(TB, pa.shape[2]), _tok_spec(TB, gb.shape[2]), _tok_spec(TB, D),
                _layer_spec(w_out, l), _layer_spec(g, l), _layer_spec(b, l)]
    return pl.pallas_call(
        functools.partial(_attn_body, alpha),
        grid=(B, n_steps),
        in_specs=in_specs,
        out_specs=_tok_spec(TB, D),
        out_shape=jax.ShapeDtypeStruct((B, S, D), F32),
        scratch_shapes=[pltpu.VMEM((n_heads, TB, TB), F32), pltpu.VMEM((n_heads, 1, TB), F32),
                        pltpu.VMEM((n_heads, V_ROWS, TB), F32)],
        compiler_params=pltpu.CompilerParams(
            dimension_semantics=("arbitrary", "arbitrary"), vmem_limit_bytes=VMEM_LIMIT),
        name="attn_out",
    )(q, k, vt, q, k, pa, gb, x, w_out, g, b)


def _ffn_body(alpha, x_ref, wu_ref, cw_ref, cb_ref, wd_ref, g_ref, b_ref, o_ref, hprev_ref):
    @pl.when(pl.program_id(1) == 0)
    def _():
        hprev_ref[...] = jnp.zeros_like(hprev_ref)

    d_ff = wd_ref.shape[0]
    x = x_ref[...]
    hconv = _causal_conv(hprev_ref, _dot(x.astype(BF16), wu_ref[...]), cw_ref, cb_ref, slice(None), FFN_CONV_WIDTH)
    hg, half_hv = hconv[:, :d_ff], hconv[:, d_ff:]
    act = ((hg * half_hv) * (1.0 + jnp.tanh(_gelu_tanh_arg(hg, 1.0)))).astype(BF16)
    o_ref[...] = _layer_norm(alpha * x + _dot(act, wd_ref[...]), g_ref[...], b_ref[...])


def _conv_ffn(l, x, w_up, cw, cb, w_down, g, b, *, tm, alpha):
    B, S, D = x.shape
    ins = [x, w_up, cw, cb, w_down, g, b]
    return pl.pallas_call(
        functools.partial(_ffn_body, alpha),
        grid=(B, S // tm),
        in_specs=[_tok_spec(tm, D)] + [_layer_spec(a, l) for a in ins[1:]],
        out_specs=_tok_spec(tm, D),
        out_shape=jax.ShapeDtypeStruct((B, S, D), F32),
        scratch_shapes=[pltpu.VMEM(((FFN_CONV_WIDTH - 1) * SUBLANES, w_up.shape[2]), F32)],
        compiler_params=pltpu.CompilerParams(
            dimension_semantics=("arbitrary", "arbitrary"), vmem_limit_bytes=VMEM_LIMIT),
        name="conv_ffn",
    )(*ins)


def _block_diag(w):
    L, H, r, _ = w.shape
    eye = jnp.eye(H, dtype=w.dtype)
    return (eye[:, None, :, None] * w[:, :, :, None, :]).reshape(L, H * r, H * r)


def _head_group(nope, rope):
    ref = nope if nope is not None else rope
    z = lambda n: jnp.zeros(ref.shape[:-1] + (n,), ref.dtype)
    r0, r1 = (z(ROPE_HALF), z(ROPE_HALF)) if rope is None else (rope[..., :ROPE_HALF], rope[..., ROPE_HALF:])
    split = HALF_GROUP - ROPE_HALF
    n0, n1 = (z(split), z(QK_NOPE - split)) if nope is None else (nope[..., :split], nope[..., split:])
    return jnp.concatenate([r0, n0, r1, n1, z(HEAD_GROUP - QK_NOPE - QK_ROPE)], axis=-1)


def _prepare_weights(rnn_chunk, w_in, conv_w, conv_b, gx_w, gx_b, ga_w, ga_b, lru_lambda,
                     q_norm_g, w_uq, kv_norm_g, w_ukv, w_out, ln1_g, ln1_b,
                     w_up, ffn_conv_w, ffn_conv_b, w_down, ln2_g, ln2_b):
    L = w_in.shape[0]
    d_rnn = conv_w.shape[2]
    q_lora, kv_lora = w_uq.shape[1], w_ukv.shape[1]
    d_ff = w_down.shape[1]
    p_q = 2 * d_rnn
    p_kv = p_q + q_lora
    p_kr = p_kv + kv_lora
    p_a = p_kr + QK_ROPE
    w_in_p = jnp.concatenate(
        [w_in[..., :d_rnn], 0.25 * w_in[..., d_rnn:p_q], 0.5 * w_in[..., p_a:], w_in[..., p_q:p_kr],
         _head_group(None, w_in[..., p_kr:p_a])], axis=-1).astype(BF16)
    gx_bd, ga_bd = _block_diag(0.5 * gx_w), _block_diag(0.5 * ga_w)
    ch = rnn_chunk
    wg = jnp.stack([
        jnp.concatenate([gx_bd[:, c * ch:(c + 1) * ch, c * ch:(c + 1) * ch],
                         ga_bd[:, c * ch:(c + 1) * ch, c * ch:(c + 1) * ch]], axis=-1)
        for c in range(d_rnn // ch)], axis=1).astype(BF16)
    row = lambda a: a[:, None, :]
    wq = w_uq.reshape(L, q_lora, N_HEADS, QK_NOPE + QK_ROPE)
    wq = _head_group(wq[..., :QK_NOPE], wq[..., QK_NOPE:]).reshape(L, q_lora, N_HEADS * HEAD_GROUP).astype(BF16)
    wkv = w_ukv.reshape(L, kv_lora, N_HEADS, QK_NOPE + V_HEAD)
    wk = _head_group(wkv[..., :QK_NOPE], None).reshape(L, kv_lora, N_HEADS * HEAD_GROUP).astype(BF16)
    wvt = jnp.pad(jnp.transpose(wkv[..., QK_NOPE:], (0, 2, 3, 1)), ((0, 0), (0, 0), (0, V_ROWS - V_HEAD), (0, 0)))
    wvt = wvt.reshape(L, N_HEADS * V_ROWS, kv_lora).astype(BF16)
    val_half = jnp.concatenate([jnp.ones((d_ff,), F32), jnp.full((d_ff,), 0.5, F32)])
    return dict(
        w_in_p=w_in_p, conv_w=conv_w, conv_b=row(conv_b), wg=wg, gx_b=row(0.5 * gx_b), ga_b=row(0.5 * ga_b),
        lam=row(lru_lambda), qg=row(q_norm_g), kvg=row(kv_norm_g), wq=wq, wk=wk, wvt=wvt,
        w_out=w_out.astype(BF16), ln1_g=row(ln1_g), ln1_b=row(ln1_b),
        w_up=w_up.astype(BF16), fcw=ffn_conv_w * val_half, fcb=row(ffn_conv_b * val_half),
        w_down=w_down.astype(BF16), ln2_g=row(ln2_g), ln2_b=row(ln2_b))


def kernel(x, positions, w_in, conv_w, conv_b, gx_w, gx_b, ga_w, ga_b, lru_lambda, q_norm_g, w_uq, kv_norm_g, w_ukv, w_out, ln1_g, ln1_b, w_up, ffn_conv_w, ffn_conv_b, w_down, ln2_g, ln2_b):
    depth = w_in.shape[0]
    S = x.shape[1]
    assert S % TB == 0
    alpha = (2 * depth) ** 0.25
    tm = 2 * TB if S % (2 * TB) == 0 else TB
    rnn_chunk = 256
    p = _prepare_weights(rnn_chunk, w_in, conv_w, conv_b, gx_w, gx_b, ga_w, ga_b, lru_lambda,
                         q_norm_g, w_uq, kv_norm_g, w_ukv, w_out, ln1_g, ln1_b,
                         w_up, ffn_conv_w, ffn_conv_b, w_down, ln2_g, ln2_b)
    x = _to_strands(x)
    cos_t, sin_t = _rope_tables(_to_strands(positions), TB)
    ones_row = (np.arange(N_HEADS * V_ROWS) % V_ROWS == V_HEAD).astype(np.float32)
    vbias = jnp.asarray(np.repeat(ones_row[:, None], LANES, axis=1))
    for l in range(depth):
        pa, gb, q, k, vt = _in_proj_rglru(
            l, x, cos_t, sin_t, p["w_in_p"], p["conv_w"], p["conv_b"], p["wg"], p["gx_b"], p["ga_b"], p["lam"],
            p["qg"], p["kvg"], p["wq"], p["wk"], p["wvt"], vbias, tm=tm, chunk=rnn_chunk)
        x = _attn_out(l, q, k, vt, pa, gb, x, p["w_out"], p["ln1_g"], p["ln1_b"], alpha=alpha)
        x = _conv_ffn(l, x, p["w_up"], p["fcw"], p["fcb"], p["w_down"], p["ln2_g"], p["ln2_b"],
                      tm=tm, alpha=alpha)
    return _from_strands(x)
```

```python
import functools
import math

import jax
import jax.numpy as jnp
import numpy as np
from jax import lax
from jax.experimental import pallas as pl
from jax.experimental.pallas import tpu as pltpu

RNN_BLOCKS = 16
CONV_WIDTH = 4
LRU_C = 8.0
N_HEADS = 16
QK_NOPE = 64
QK_ROPE = 32
V_HEAD = 64
ROPE_THETA = 10000.0
FFN_CONV_WIDTH = 3
EPS = 1e-6
NEG_INF = -1e30

LANES = 128
SUBLANES = 8
TB = 256
ROWS = TB // SUBLANES
HEAD_GROUP = LANES
ROPE_HALF = QK_ROPE // 2
HALF_GROUP = HEAD_GROUP // 2
V_ROWS = 80
VMEM_LIMIT = 56 * 1024 * 1024
SCORE_SCALE = (QK_NOPE + QK_ROPE) ** -0.5 * math.log2(math.e)

BF16 = jnp.bfloat16
F32 = jnp.float32


def _dot(a, b):
    return jnp.dot(a, b, preferred_element_type=F32)


GELU_K0 = math.sqrt(2.0 / math.pi)
GELU_K1 = 0.044715


def _gelu_tanh_arg(x, scale):
    return x * (scale * GELU_K0 + (scale ** 3 * GELU_K0 * GELU_K1) * (x * x))


def _sqrt_nonneg(x):
    return jnp.where(x > 0.0, x * lax.rsqrt(x), 0.0)


def _layer_norm(z, g, b):
    mu = jnp.mean(z, axis=-1, keepdims=True)
    zc = z - mu
    var = jnp.mean(zc * zc, axis=-1, keepdims=True)
    return zc * lax.rsqrt(var + EPS) * g + b


def _rms_norm(z, g):
    return z * lax.rsqrt(jnp.mean(z * z, axis=-1, keepdims=True) + EPS) * g


def _to_strands(a):
    B, S = a.shape[:2]
    rest = a.shape[2:]
    return a.reshape(B, S // TB, SUBLANES, ROWS, *rest).swapaxes(2, 3).reshape(B, S, *rest)


def _from_strands(a):
    B, S = a.shape[:2]
    rest = a.shape[2:]
    return a.reshape(B, S // TB, ROWS, SUBLANES, *rest).swapaxes(2, 3).reshape(B, S, *rest)


def _strand_time(r):
    return (r & (SUBLANES - 1)) * ROWS + (r >> 3)


def _delays(prev_tail, x, depth):
    n = depth * SUBLANES
    C = x.shape[1]
    sub = lax.broadcasted_iota(jnp.int32, (depth, SUBLANES, C), 1)
    own = pltpu.roll(x[TB - n:].reshape(depth, SUBLANES, C), 1, 1)
    prev = pltpu.roll(prev_tail.reshape(depth, SUBLANES, C), 1, 1)
    head = jnp.where(sub == 0, prev, own).reshape(n, C)
    ext = jnp.concatenate([head, x], axis=0)
    return [ext[(depth - k) * SUBLANES:(depth - k) * SUBLANES + TB] for k in range(1, depth + 1)]


def _blocks(z):
    return [z[i * TB:(i + 1) * TB] for i in range(z.shape[0] // TB)]


def _causal_conv(prev_ref, z, w_ref, b_ref, cs, width):
    tail = prev_ref.shape[0]
    prev = prev_ref[:, cs]
    outs = []
    for x in _blocks(z):
        out = x * w_ref[width - 1:width, cs] + b_ref[:, cs]
        for k, xd in enumerate(_delays(prev, x, width - 1), start=1):
            out = out + xd * w_ref[width - 1 - k:width - k, cs]
        outs.append(out)
        prev = x[TB - tail:]
    prev_ref[:, cs] = prev
    return jnp.concatenate(outs, axis=0)


def _layer_spec(a, l):
    nd = a.ndim - 1
    return pl.BlockSpec((None,) + a.shape[1:], lambda *_: (l,) + (0,) * nd, pipeline_mode=pl.Buffered(1))


def _tok_spec(tm, width):
    return pl.BlockSpec((None, tm, width), lambda b, s: (b, s, 0))


def _rope_tables_body(pos_ref, invf_ref, cos_ref, sin_ref):
    pos = pos_ref[...].astype(F32)
    ang = pos * invf_ref[...]
    lane = lax.broadcasted_iota(jnp.int32, ang.shape, 1)
    cosv = jnp.cos(ang)
    sinv = jnp.sin(ang)
    rot = (lane & (HALF_GROUP - 1)) < ROPE_HALF
    cos_ref[...] = jnp.where(rot, cosv, 1.0)
    sin_ref[...] = jnp.where(rot, jnp.where(lane < HALF_GROUP, -sinv, sinv), 0.0)


def _rope_tables(positions, tm):
    B, S = positions.shape
    inv_freq = ROPE_THETA ** (-jnp.arange(0, QK_ROPE, 2, dtype=F32) / QK_ROPE)
    idx = np.minimum(np.arange(LANES) % HALF_GROUP, ROPE_HALF - 1)
    invf = inv_freq[idx].reshape(1, LANES)
    pos = positions.reshape(B, S, 1)
    out = jax.ShapeDtypeStruct((B, S, LANES), F32)
    return pl.pallas_call(
        _rope_tables_body,
        grid=(B, S // tm),
        in_specs=[_tok_spec(tm, 1), pl.BlockSpec((1, LANES), lambda b, s: (0, 0))],
        out_specs=[_tok_spec(tm, LANES), _tok_spec(tm, LANES)],
        out_shape=[out, out],
        compiler_params=pltpu.CompilerParams(dimension_semantics=("parallel", "parallel")),
        name="rope_tables",
    )(pos, invf)


def _lru_scan(a, u, h0):
    C = a.shape[1]
    row = lambda z, v: z[v * SUBLANES:(v + 1) * SUBLANES]
    p, h = row(a, 0), row(u, 0)
    ps, hs = [p], [h]
    for v in range(1, ROWS):
        av = row(a, v)
        p = av * p
        h = av * h + row(u, v)
        ps.append(p)
        hs.append(h)
    sub = lax.broadcasted_iota(jnp.int32, (SUBLANES, C), 0)
    pc, hc = p, h
    step = 1
    while step < SUBLANES:
        m = sub >= step
        rp = pltpu.roll(pc, step, 0)
        rh = pltpu.roll(hc, step, 0)
        hc = jnp.where(m, pc * rh + hc, hc)
        pc = jnp.where(m, pc * rp, pc)
        step *= 2
    end = hc + pc * h0
    init = jnp.where(sub == 0, h0, pltpu.roll(end, 1, 0))
    out = jnp.concatenate([hs[v] + ps[v] * init for v in range(ROWS)], axis=0)
    return out, end[SUBLANES - 1:]


def _in_proj_body(d_rnn, d_mix, q_lora, kv_lora, chunk,
                  x_ref, cos_ref, sin_ref, w_ref, cw_ref, cb_ref, wg_ref, gxb_ref, gab_ref, lam_ref, qg_ref, kvg_ref,
                  wq_ref, wk_ref, wvt_ref, vbias_ref,
                  pa_ref, gb_ref, q_ref, k_ref, vt_ref, xprev_ref, hc_ref):
    @pl.when(pl.program_id(1) == 0)
    def _():
        xprev_ref[...] = jnp.zeros_like(xprev_ref)
        hc_ref[...] = jnp.zeros_like(hc_ref)

    xb = x_ref[...].astype(BF16)
    neg_lam = -lam_ref[...]
    softplus = jnp.maximum(neg_lam, 0.0) + jnp.log1p(jnp.exp(-jnp.abs(neg_lam)))
    half_rate = (-0.5 * LRU_C) * softplus
    off_g, off_a, off_b = d_rnn, 2 * d_rnn, 2 * d_rnn + d_mix
    off_q = 2 * d_rnn + 2 * d_mix
    off_kv = off_q + q_lora
    off_kpe = off_kv + kv_lora

    for c in range(d_rnn // chunk):
        cs = slice(c * chunk, (c + 1) * chunk)
        xr = _dot(xb, w_ref[:, cs])
        conv = _causal_conv(xprev_ref, xr, cw_ref, cb_ref, cs, CONV_WIDTH)

        gates = _dot(conv.astype(BF16), wg_ref[c])
        tx = jnp.tanh(gates[:, :chunk] + gxb_ref[:, cs])
        tr = jnp.tanh(gates[:, chunk:] + gab_ref[:, cs])
        log_a = half_rate[:, cs] * tr + half_rate[:, cs]
        a = jnp.exp(log_a)
        t = jnp.tanh(log_a)
        half_mult = _sqrt_nonneg((-0.5 * t) / (1.0 - t))
        u = (half_mult * conv) * (tx + 1.0)
        hs, hlast = [], hc_ref[:, cs]
        for a_blk, u_blk in zip(_blocks(a), _blocks(u)):
            h_blk, hlast = _lru_scan(a_blk, u_blk, hlast)
            hs.append(h_blk)
        h = jnp.concatenate(hs, axis=0)
        hc_ref[:, cs] = hlast

        gq = _dot(xb, w_ref[:, off_g + c * chunk:off_g + (c + 1) * chunk])
        tg = jnp.tanh(_gelu_tanh_arg(gq, 4.0))
        ta = jnp.tanh(_dot(xb, w_ref[:, off_a + c * chunk:off_a + (c + 1) * chunk]))
        pa_ref[:, cs] = ((1.0 + tg) * (1.0 + ta)) * (gq * h)
        gb_ref[:, cs] = _dot(xb, w_ref[:, off_b + c * chunk:off_b + (c + 1) * chunk])

    cosv = cos_ref[...]
    sinv = sin_ref[...]

    def rope(z):
        return z * cosv + pltpu.roll(z, HALF_GROUP, 1) * sinv

    qn = _rms_norm(_dot(xb, w_ref[:, off_q:off_kv]), qg_ref[...]).astype(BF16)
    kvn = _rms_norm(_dot(xb, w_ref[:, off_kv:off_kpe]), kvg_ref[...]).astype(BF16)
    kpe = rope(_dot(xb, w_ref[:, off_kpe:off_kpe + HEAD_GROUP]))
    qf = _dot(qn, wq_ref[...])
    kf = _dot(kvn, wk_ref[...])
    for h in range(qf.shape[1] // HEAD_GROUP):
        sl = slice(h * HEAD_GROUP, (h + 1) * HEAD_GROUP)
        q_ref[:, sl] = (rope(qf[:, sl]) * SCORE_SCALE).astype(BF16)
        k_ref[:, sl] = (kf[:, sl] + kpe).astype(BF16)
    vbias = vbias_ref[...]
    for i, kvn_blk in enumerate(_blocks(kvn)):
        vt = lax.dot_general(wvt_ref[...], kvn_blk, (((1,), (1,)), ((), ())), preferred_element_type=F32)
        for c in range(TB // LANES):
            sl = slice(c * LANES, (c + 1) * LANES)
            vt_ref[i, :, sl] = (vt[:, sl] + vbias).astype(BF16)


def _in_proj_rglru(l, x, cos_t, sin_t, w_in_p, conv_w, conv_b, wg, gx_b, ga_b, lam, qg, kvg, wq, wk, wvt, vbias,
                   *, tm, chunk):
    B, S, D = x.shape
    d_rnn = conv_w.shape[2]
    q_lora, kv_lora = qg.shape[2], kvg.shape[2]
    d_mix = (w_in_p.shape[2] - 2 * d_rnn - q_lora - kv_lora - HEAD_GROUP) // 2
    body = functools.partial(_in_proj_body, d_rnn, d_mix, q_lora, kv_lora, chunk)
    layer_params = [w_in_p, conv_w, conv_b, wg, gx_b, ga_b, lam, qg, kvg, wq, wk, wvt]
    in_specs = ([_tok_spec(tm, D), _tok_spec(tm, LANES), _tok_spec(tm, LANES)]
                + [_layer_spec(a, l) for a in layer_params]
                + [pl.BlockSpec(vbias.shape, lambda b, s: (0, 0), pipeline_mode=pl.Buffered(1))])
    rows = wvt.shape[1]
    return pl.pallas_call(
        body,
        grid=(B, S // tm),
        in_specs=in_specs,
        out_specs=[_tok_spec(tm, d_mix), _tok_spec(tm, d_mix), _tok_spec(tm, wq.shape[2]), _tok_spec(tm, wk.shape[2]),
                   pl.BlockSpec((None, tm // TB, rows, TB), lambda b, s: (b, s, 0, 0))],
        out_shape=[jax.ShapeDtypeStruct((B, S, d_mix), F32), jax.ShapeDtypeStruct((B, S, d_mix), F32),
                   jax.ShapeDtypeStruct((B, S, wq.shape[2]), BF16), jax.ShapeDtypeStruct((B, S, wk.shape[2]), BF16),
                   jax.ShapeDtypeStruct((B, S // TB, rows, TB), BF16)],
        scratch_shapes=[pltpu.VMEM(((CONV_WIDTH - 1) * SUBLANES, d_rnn), F32), pltpu.VMEM((1, d_rnn), F32)],
        compiler_params=pltpu.CompilerParams(
            dimension_semantics=("arbitrary", "arbitrary"), vmem_limit_bytes=VMEM_LIMIT),
        name="in_proj_rglru",
    )(x, cos_t, sin_t, *layer_params, vbias)


def _attn_body(alpha, q_ref, k_ref, vt_ref, qn_ref, kn_ref, pa_ref, gb_ref, x_ref, wo_ref, g_ref, b_ref, o_ref,
               s_ref, m_ref, acc_ref):
    n_heads = m_ref.shape[0]
    qi = pl.program_id(1)
    key_t = _strand_time(lax.broadcasted_iota(jnp.int32, (TB, TB), 0))
    qry_t = _strand_time(lax.broadcasted_iota(jnp.int32, (TB, TB), 1))
    mask_bias = jnp.where(key_t <= qry_t, 0.0, NEG_INF)
    m_ref[...] = jnp.full(m_ref.shape, NEG_INF, F32)
    acc_ref[...] = jnp.zeros(acc_ref.shape, F32)
    head = lambda h: slice(h * HEAD_GROUP, (h + 1) * HEAD_GROUP)

    def scores(k_blk, q_blk):
        return lax.dot_general(k_blk, q_blk, (((1,), (1,)), ((), ())), preferred_element_type=F32)

    def attend(j, h, s):
        m_old = m_ref[h]
        m_new = jnp.maximum(m_old, jnp.max(s, axis=0, keepdims=True))
        p = jnp.exp2(s - m_new).astype(BF16)
        pv = _dot(vt_ref[j, h * V_ROWS:(h + 1) * V_ROWS, :], p)
        acc_ref[h] = acc_ref[h] * jnp.exp2(m_old - m_new) + pv
        m_ref[h] = m_new

    @pl.when((pl.program_id(0) == 0) & (qi == 0))
    def _():
        for h in range(n_heads):
            s_ref[h] = scores(k_ref[0:TB, head(h)], q_ref[:, head(h)])

    def body(j, carry):
        off = pl.multiple_of((j + 1) * TB, TB)
        for h in range(n_heads):
            s = s_ref[h]
            s_ref[h] = scores(k_ref[pl.ds(off, TB), head(h)], q_ref[:, head(h)])
            attend(j, h, s)
        return carry

    def body_pair(t, carry):
        return body(2 * t + 1, body(2 * t, carry))

    lax.fori_loop(0, qi // 2, body_pair, 0)

    @pl.when(qi % 2 == 1)
    def _():
        body(qi - 1, 0)

    for h in range(n_heads):
        s = s_ref[h]
        s_ref[h] = scores(kn_ref[:, head(h)], qn_ref[:, head(h)])
        attend(qi, h, s + mask_bias)

    ys = []
    for h in range(n_heads):
        a = acc_ref[h]
        ys.append(a[:V_HEAD] / a[V_HEAD:V_HEAD + 1])
    y = jnp.concatenate(ys, axis=0).T
    merged = pa_ref[...] + (0.5 * jnp.tanh(gb_ref[...]) + 0.5) * y
    o = _dot(merged.astype(BF16), wo_ref[...])
    o_ref[...] = _layer_norm(alpha * x_ref[...] + o, g_ref[...], b_ref[...])


def _attn_out(l, q, k, vt, pa, gb, x, w_out, g, b, *, alpha):
    B, S, D = x.shape
    n_heads = k.shape[2] // HEAD_GROUP
    n_steps = S // TB
    assert vt.shape[3] == TB and vt.shape[2] == n_heads * V_ROWS

    def next_step(bi, s):
        wrap = s + 1 == n_steps
        return jnp.where(wrap, jnp.minimum(bi + 1, B - 1), bi), jnp.where(wrap, 0, s + 1)

    in_specs = [_tok_spec(TB, q.shape[2]),
                pl.BlockSpec((None, S, k.shape[2]), lambda bi, s: (bi, 0, 0)),
                pl.BlockSpec((None,) + vt.shape[1:], lambda bi, s: (bi, 0, 0, 0)),
                pl.BlockSpec((None, TB, q.shape[2]), lambda bi, s: (*next_step(bi, s), 0)),
                pl.BlockSpec((None, TB, k.shape[2]), lambda bi, s: (next_step(bi, s)[0], 0, 0)),
                _tok_spec(TB, pa.shape[2]), _tok_spec(TB, gb.shape[2]), _tok_spec(TB, D),
                _layer_spec(w_out, l), _layer_spec(g, l), _layer_spec(b, l)]
    return pl.pallas_call(
        functools.partial(_attn_body, alpha),
        grid=(B, n_steps),
        in_specs=in_specs,
        out_specs=_tok_spec(TB, D),
        out_shape=jax.ShapeDtypeStruct((B, S, D), F32),
        scratch_shapes=[pltpu.VMEM((n_heads, TB, TB), F32), pltpu.VMEM((n_heads, 1, TB), F32),
                        pltpu.VMEM((n_heads, V_ROWS, TB), F32)],
        compiler_params=pltpu.CompilerParams(
            dimension_semantics=("arbitrary", "arbitrary"), vmem_limit_bytes=VMEM_LIMIT),
        name="attn_out",
    )(q, k, vt, q, k, pa, gb, x, w_out, g, b)


def _ffn_body(alpha, x_ref, wu_ref, cw_ref, cb_ref, wd_ref, g_ref, b_ref, o_ref, hprev_ref):
    @pl.when(pl.program_id(1) == 0)
    def _():
        hprev_ref[...] = jnp.zeros_like(hprev_ref)

    d_ff = wd_ref.shape[0]
    x = x_ref[...]
    hconv = _causal_conv(hprev_ref, _dot(x.astype(BF16), wu_ref[...]), cw_ref, cb_ref, slice(None), FFN_CONV_WIDTH)
    hg, half_hv = hconv[:, :d_ff], hconv[:, d_ff:]
    act = ((hg * half_hv) * (1.0 + jnp.tanh(_gelu_tanh_arg(hg, 1.0)))).astype(BF16)
    o_ref[...] = _layer_norm(alpha * x + _dot(act, wd_ref[...]), g_ref[...], b_ref[...])


def _conv_ffn(l, x, w_up, cw, cb, w_down, g, b, *, tm, alpha):
    B, S, D = x.shape
    ins = [x, w_up, cw, cb, w_down, g, b]
    return pl.pallas_call(
        functools.partial(_ffn_body, alpha),
        grid=(B, S // tm),
        in_specs=[_tok_spec(tm, D)] + [_layer_spec(a, l) for a in ins[1:]],
        out_specs=_tok_spec(tm, D),
        out_shape=jax.ShapeDtypeStruct((B, S, D), F32),
        scratch_shapes=[pltpu.VMEM(((FFN_CONV_WIDTH - 1) * SUBLANES, w_up.shape[2]), F32)],
        compiler_params=pltpu.CompilerParams(
            dimension_semantics=("arbitrary", "arbitrary"), vmem_limit_bytes=VMEM_LIMIT),
        name="conv_ffn",
    )(*ins)


def _block_diag(w):
    L, H, r, _ = w.shape
    eye = jnp.eye(H, dtype=w.dtype)
    return (eye[:, None, :, None] * w[:, :, :, None, :]).reshape(L, H * r, H * r)


def _head_group(nope, rope):
    ref = nope if nope is not None else rope
    z = lambda n: jnp.zeros(ref.shape[:-1] + (n,), ref.dtype)
    r0, r1 = (z(ROPE_HALF), z(ROPE_HALF)) if rope is None else (rope[..., :ROPE_HALF], rope[..., ROPE_HALF:])
    split = HALF_GROUP - ROPE_HALF
    n0, n1 = (z(split), z(QK_NOPE - split)) if nope is None else (nope[..., :split], nope[..., split:])
    return jnp.concatenate([r0, n0, r1, n1, z(HEAD_GROUP - QK_NOPE - QK_ROPE)], axis=-1)


def _prepare_weights(rnn_chunk, w_in, conv_w, conv_b, gx_w, gx_b, ga_w, ga_b, lru_lambda,
                     q_norm_g, w_uq, kv_norm_g, w_ukv, w_out, ln1_g, ln1_b,
                     w_up, ffn_conv_w, ffn_conv_b, w_down, ln2_g, ln2_b):
    L = w_in.shape[0]
    d_rnn = conv_w.shape[2]
    q_lora, kv_lora = w_uq.shape[1], w_ukv.shape[1]
    d_ff = w_down.shape[1]
    p_q = 2 * d_rnn
    p_kv = p_q + q_lora
    p_kr = p_kv + kv_lora
    p_a = p_kr + QK_ROPE
    w_in_p = jnp.concatenate(
        [w_in[..., :d_rnn], 0.25 * w_in[..., d_rnn:p_q], 0.5 * w_in[..., p_a:], w_in[..., p_q:p_kr],
         _head_group(None, w_in[..., p_kr:p_a])], axis=-1).astype(BF16)
    gx_bd, ga_bd = _block_diag(0.5 * gx_w), _block_diag(0.5 * ga_w)
    ch = rnn_chunk
    wg = jnp.stack([
        jnp.concatenate([gx_bd[:, c * ch:(c + 1) * ch, c * ch:(c + 1) * ch],
                         ga_bd[:, c * ch:(c + 1) * ch, c * ch:(c + 1) * ch]], axis=-1)
        for c in range(d_rnn // ch)], axis=1).astype(BF16)
    row = lambda a: a[:, None, :]
    wq = w_uq.reshape(L, q_lora, N_HEADS, QK_NOPE + QK_ROPE)
    wq = _head_group(wq[..., :QK_NOPE], wq[..., QK_NOPE:]).reshape(L, q_lora, N_HEADS * HEAD_GROUP).astype(BF16)
    wkv = w_ukv.reshape(L, kv_lora, N_HEADS, QK_NOPE + V_HEAD)
    wk = _head_group(wkv[..., :QK_NOPE], None).reshape(L, kv_lora, N_HEADS * HEAD_GROUP).astype(BF16)
    wvt = jnp.pad(jnp.transpose(wkv[..., QK_NOPE:], (0, 2, 3, 1)), ((0, 0), (0, 0), (0, V_ROWS - V_HEAD), (0, 0)))
    wvt = wvt.reshape(L, N_HEADS * V_ROWS, kv_lora).astype(BF16)
    val_half = jnp.concatenate([jnp.ones((d_ff,), F32), jnp.full((d_ff,), 0.5, F32)])
    return dict(
        w_in_p=w_in_p, conv_w=conv_w, conv_b=row(conv_b), wg=wg, gx_b=row(0.5 * gx_b), ga_b=row(0.5 * ga_b),
        lam=row(lru_lambda), qg=row(q_norm_g), kvg=row(kv_norm_g), wq=wq, wk=wk, wvt=wvt,
        w_out=w_out.astype(BF16), ln1_g=row(ln1_g), ln1_b=row(ln1_b),
        w_up=w_up.astype(BF16), fcw=ffn_conv_w * val_half, fcb=row(ffn_conv_b * val_half),
        w_down=w_down.astype(BF16), ln2_g=row(ln2_g), ln2_b=row(ln2_b))


def kernel(x, positions, w_in, conv_w, conv_b, gx_w, gx_b, ga_w, ga_b, lru_lambda, q_norm_g, w_uq, kv_norm_g, w_ukv, w_out, ln1_g, ln1_b, w_up, ffn_conv_w, ffn_conv_b, w_down, ln2_g, ln2_b):
    depth = w_in.shape[0]
    S = x.shape[1]
    assert S % TB == 0
    alpha = (2 * depth) ** 0.25
    tm = 2 * TB if S % (2 * TB) == 0 else TB
    rnn_chunk = 256
    p = _prepare_weights(rnn_chunk, w_in, conv_w, conv_b, gx_w, gx_b, ga_w, ga_b, lru_lambda,
                         q_norm_g, w_uq, kv_norm_g, w_ukv, w_out, ln1_g, ln1_b,
                         w_up, ffn_conv_w, ffn_conv_b, w_down, ln2_g, ln2_b)
    x = _to_strands(x)
    cos_t, sin_t = _rope_tables(_to_strands(positions), S)
    ones_row = (np.arange(N_HEADS * V_ROWS) % V_ROWS == V_HEAD).astype(np.float32)
    vbias = jnp.asarray(np.repeat(ones_row[:, None], LANES, axis=1))
    for l in range(depth):
        pa, gb, q, k, vt = _in_proj_rglru(
            l, x, cos_t, sin_t, p["w_in_p"], p["conv_w"], p["conv_b"], p["wg"], p["gx_b"], p["ga_b"], p["lam"],
            p["qg"], p["kvg"], p["wq"], p["wk"], p["wvt"], vbias, tm=tm, chunk=rnn_chunk)
        x = _attn_out(l, q, k, vt, pa, gb, x, p["w_out"], p["ln1_g"], p["ln1_b"], alpha=alpha)
        x = _conv_ffn(l, x, p["w_up"], p["fcw"], p["fcb"], p["w_down"], p["ln2_g"], p["ln2_b"],
                      tm=tm, alpha=alpha)
    return _from_strands(x)
```

```python
import functools
import math

import jax
import jax.numpy as jnp
import numpy as np
from jax import lax
from jax.experimental import pallas as pl
from jax.experimental.pallas import tpu as pltpu

RNN_BLOCKS = 16
CONV_WIDTH = 4
LRU_C = 8.0
N_HEADS = 16
QK_NOPE = 64
QK_ROPE = 32
V_HEAD = 64
ROPE_THETA = 10000.0
FFN_CONV_WIDTH = 3
EPS = 1e-6
NEG_INF = -1e30

LANES = 128
SUBLANES = 8
TB = 256
ROWS = TB // SUBLANES
HEAD_GROUP = LANES
ROPE_HALF = QK_ROPE // 2
HALF_GROUP = HEAD_GROUP // 2
V_ROWS = 80
VMEM_LIMIT = 56 * 1024 * 1024
SCORE_SCALE = (QK_NOPE + QK_ROPE) ** -0.5 * math.log2(math.e)

BF16 = jnp.bfloat16
F32 = jnp.float32


def _dot(a, b):
    return jnp.dot(a, b, preferred_element_type=F32)


GELU_K0 = math.sqrt(2.0 / math.pi)
GELU_K1 = 0.044715


def _gelu_tanh_arg(x, scale):
    return x * (scale * GELU_K0 + (scale ** 3 * GELU_K0 * GELU_K1) * (x * x))


def _sqrt_nonneg(x):
    return jnp.where(x > 0.0, x * lax.rsqrt(x), 0.0)


def _layer_norm(z, g, b):
    mu = jnp.mean(z, axis=-1, keepdims=True)
    zc = z - mu
    var = jnp.mean(zc * zc, axis=-1, keepdims=True)
    return zc * lax.rsqrt(var + EPS) * g + b


def _rms_norm(z, g):
    return z * lax.rsqrt(jnp.mean(z * z, axis=-1, keepdims=True) + EPS) * g


def _to_strands(a):
    B, S = a.shape[:2]
    rest = a.shape[2:]
    return a.reshape(B, S // TB, SUBLANES, ROWS, *rest).swapaxes(2, 3).reshape(B, S, *rest)


def _from_strands(a):
    B, S = a.shape[:2]
    rest = a.shape[2:]
    return a.reshape(B, S // TB, ROWS, SUBLANES, *rest).swapaxes(2, 3).reshape(B, S, *rest)


def _strand_time(r):
    return (r & (SUBLANES - 1)) * ROWS + (r >> 3)


def _delays(prev_tail, x, depth):
    n = depth * SUBLANES
    C = x.shape[1]
    sub = lax.broadcasted_iota(jnp.int32, (depth, SUBLANES, C), 1)
    own = pltpu.roll(x[TB - n:].reshape(depth, SUBLANES, C), 1, 1)
    prev = pltpu.roll(prev_tail.reshape(depth, SUBLANES, C), 1, 1)
    head = jnp.where(sub == 0, prev, own).reshape(n, C)
    ext = jnp.concatenate([head, x], axis=0)
    return [ext[(depth - k) * SUBLANES:(depth - k) * SUBLANES + TB] for k in range(1, depth + 1)]


def _blocks(z):
    return [z[i * TB:(i + 1) * TB] for i in range(z.shape[0] // TB)]


def _causal_conv(prev_ref, z, w_ref, b_ref, cs, width):
    tail = prev_ref.shape[0]
    prev = prev_ref[:, cs]
    outs = []
    for x in _blocks(z):
        out = x * w_ref[width - 1:width, cs] + b_ref[:, cs]
        for k, xd in enumerate(_delays(prev, x, width - 1), start=1):
            out = out + xd * w_ref[width - 1 - k:width - k, cs]
        outs.append(out)
        prev = x[TB - tail:]
    prev_ref[:, cs] = prev
    return jnp.concatenate(outs, axis=0)


def _layer_spec(a, l):
    nd = a.ndim - 1
    return pl.BlockSpec((None,) + a.shape[1:], lambda *_: (l,) + (0,) * nd, pipeline_mode=pl.Buffered(1))


def _tok_spec(tm, width):
    return pl.BlockSpec((None, tm, width), lambda b, s: (b, s, 0))


def _rope_tables_body(pos_ref, invf_ref, cos_ref, sin_ref):
    pos = pos_ref[...].astype(F32)
    ang = pos * invf_ref[...]
    lane = lax.broadcasted_iota(jnp.int32, ang.shape, 1)
    cosv = jnp.cos(ang)
    sinv = jnp.sin(ang)
    rot = (lane & (HALF_GROUP - 1)) < ROPE_HALF
    cos_ref[...] = jnp.where(rot, cosv, 1.0)
    sin_ref[...] = jnp.where(rot, jnp.where(lane < HALF_GROUP, -sinv, sinv), 0.0)


def _rope_tables(positions, tm):
    B, S = positions.shape
    inv_freq = ROPE_THETA ** (-jnp.arange(0, QK_ROPE, 2, dtype=F32) / QK_ROPE)
    idx = np.minimum(np.arange(LANES) % HALF_GROUP, ROPE_HALF - 1)
    invf = inv_freq[idx].reshape(1, LANES)
    pos = positions.reshape(B, S, 1)
    out = jax.ShapeDtypeStruct((B, S, LANES), F32)
    return pl.pallas_call(
        _rope_tables_body,
        grid=(B, S // tm),
        in_specs=[_tok_spec(tm, 1), pl.BlockSpec((1, LANES), lambda b, s: (0, 0))],
        out_specs=[_tok_spec(tm, LANES), _tok_spec(tm, LANES)],
        out_shape=[out, out],
        compiler_params=pltpu.CompilerParams(dimension_semantics=("parallel", "parallel")),
        name="rope_tables",
    )(pos, invf)


def _lru_scan(a, u, h0):
    C = a.shape[1]
    row = lambda z, v: z[v * SUBLANES:(v + 1) * SUBLANES]
    p, h = row(a, 0), row(u, 0)
    ps, hs = [p], [h]
    for v in range(1, ROWS):
        av = row(a, v)
        p = av * p
        h = av * h + row(u, v)
        ps.append(p)
        hs.append(h)
    sub = lax.broadcasted_iota(jnp.int32, (SUBLANES, C), 0)
    pc, hc = p, h
    step = 1
    while step < SUBLANES:
        m = sub >= step
        rp = pltpu.roll(pc, step, 0)
        rh = pltpu.roll(hc, step, 0)
        hc = jnp.where(m, pc * rh + hc, hc)
        pc = jnp.where(m, pc * rp, pc)
        step *= 2
    end = hc + pc * h0
    init = jnp.where(sub == 0, h0, pltpu.roll(end, 1, 0))
    out = jnp.concatenate([hs[v] + ps[v] * init for v in range(ROWS)], axis=0)
    return out, end[SUBLANES - 1:]


def _in_proj_body(d_rnn, d_mix, q_lora, kv_lora, chunk,
                  x_ref, cos_ref, sin_ref, w_ref, cw_ref, cb_ref, wg_ref, gxb_ref, gab_ref, lam_ref, qg_ref, kvg_ref,
                  wq_ref, wk_ref, wvt_ref, vbias_ref,
                  pa_ref, gb_ref, q_ref, k_ref, vt_ref, xprev_ref, hc_ref):
    @pl.when(pl.program_id(1) == 0)
    def _():
        xprev_ref[...] = jnp.zeros_like(xprev_ref)
        hc_ref[...] = jnp.zeros_like(hc_ref)

    xb = x_ref[...].astype(BF16)
    neg_lam = -lam_ref[...]
    softplus = jnp.maximum(neg_lam, 0.0) + jnp.log1p(jnp.exp(-jnp.abs(neg_lam)))
    half_rate = (-0.5 * LRU_C) * softplus
    off_g, off_a, off_b = d_rnn, 2 * d_rnn, 2 * d_rnn + d_mix
    off_q = 2 * d_rnn + 2 * d_mix
    off_kv = off_q + q_lora
    off_kpe = off_kv + kv_lora

    for c in range(d_rnn // chunk):
        cs = slice(c * chunk, (c + 1) * chunk)
        xr = _dot(xb, w_ref[:, cs])
        conv = _causal_conv(xprev_ref, xr, cw_ref, cb_ref, cs, CONV_WIDTH)

        gates = _dot(conv.astype(BF16), wg_ref[c])
        tx = jnp.tanh(gates[:, :chunk] + gxb_ref[:, cs])
        tr = jnp.tanh(gates[:, chunk:] + gab_ref[:, cs])
        log_a = half_rate[:, cs] * tr + half_rate[:, cs]
        a = jnp.exp(log_a)
        t = jnp.tanh(log_a)
        half_mult = _sqrt_nonneg((-0.5 * t) / (1.0 - t))
        u = (half_mult * conv) * (tx + 1.0)
        hs, hlast = [], hc_ref[:, cs]
        for a_blk, u_blk in zip(_blocks(a), _blocks(u)):
            h_blk, hlast = _lru_scan(a_blk, u_blk, hlast)
            hs.append(h_blk)
        h = jnp.concatenate(hs, axis=0)
        hc_ref[:, cs] = hlast

        gq = _dot(xb, w_ref[:, off_g + c * chunk:off_g + (c + 1) * chunk])
        tg = jnp.tanh(_gelu_tanh_arg(gq, 4.0))
        ta = jnp.tanh(_dot(xb, w_ref[:, off_a + c * chunk:off_a + (c + 1) * chunk]))
        pa_ref[:, cs] = ((1.0 + tg) * (1.0 + ta)) * (gq * h)
        gb_ref[:, cs] = _dot(xb, w_ref[:, off_b + c * chunk:off_b + (c + 1) * chunk])

    cosv = cos_ref[...]
    sinv = sin_ref[...]

    def rope(z):
        return z * cosv + pltpu.roll(z, HALF_GROUP, 1) * sinv

    qn = _rms_norm(_dot(xb, w_ref[:, off_q:off_kv]), qg_ref[...]).astype(BF16)
    kvn = _rms_norm(_dot(xb, w_ref[:, off_kv:off_kpe]), kvg_ref[...]).astype(BF16)
    kpe = rope(_dot(xb, w_ref[:, off_kpe:off_kpe + HEAD_GROUP]))
    qf = _dot(qn, wq_ref[...])
    kf = _dot(kvn, wk_ref[...])
    for h in range(qf.shape[1] // HEAD_GROUP):
        sl = slice(h * HEAD_GROUP, (h + 1) * HEAD_GROUP)
        q_ref[:, sl] = (rope(qf[:, sl]) * SCORE_SCALE).astype(BF16)
        k_ref[:, sl] = (kf[:, sl] + kpe).astype(BF16)
    vbias = vbias_ref[...]
    for i, kvn_blk in enumerate(_blocks(kvn)):
        vt = lax.dot_general(wvt_ref[...], kvn_blk, (((1,), (1,)), ((), ())), preferred_element_type=F32)
        for c in range(TB // LANES):
            sl = slice(c * LANES, (c + 1) * LANES)
            vt_ref[i, :, sl] = (vt[:, sl] + vbias).astype(BF16)


def _in_proj_rglru(l, x, cos_t, sin_t, w_in_p, conv_w, conv_b, wg, gx_b, ga_b, lam, qg, kvg, wq, wk, wvt, vbias,
                   *, tm, chunk):
    B, S, D = x.shape
    d_rnn = conv_w.shape[2]
    q_lora, kv_lora = qg.shape[2], kvg.shape[2]
    d_mix = (w_in_p.shape[2] - 2 * d_rnn - q_lora - kv_lora - HEAD_GROUP) // 2
    body = functools.partial(_in_proj_body, d_rnn, d_mix, q_lora, kv_lora, chunk)
    layer_params = [w_in_p, conv_w, conv_b, wg, gx_b, ga_b, lam, qg, kvg, wq, wk, wvt]
    in_specs = ([_tok_spec(tm, D), _tok_spec(tm, LANES), _tok_spec(tm, LANES)]
                + [_layer_spec(a, l) for a in layer_params]
                + [pl.BlockSpec(vbias.shape, lambda b, s: (0, 0), pipeline_mode=pl.Buffered(1))])
    rows = wvt.shape[1]
    return pl.pallas_call(
        body,
        grid=(B, S // tm),
        in_specs=in_specs,
        out_specs=[_tok_spec(tm, d_mix), _tok_spec(tm, d_mix), _tok_spec(tm, wq.shape[2]), _tok_spec(tm, wk.shape[2]),
                   pl.BlockSpec((None, tm // TB, rows, TB), lambda b, s: (b, s, 0, 0))],
        out_shape=[jax.ShapeDtypeStruct((B, S, d_mix), F32), jax.ShapeDtypeStruct((B, S, d_mix), F32),
                   jax.ShapeDtypeStruct((B, S, wq.shape[2]), BF16), jax.ShapeDtypeStruct((B, S, wk.shape[2]), BF16),
                   jax.ShapeDtypeStruct((B, S // TB, rows, TB), BF16)],
        scratch_shapes=[pltpu.VMEM(((CONV_WIDTH - 1) * SUBLANES, d_rnn), F32), pltpu.VMEM((1, d_rnn), F32)],
        compiler_params=pltpu.CompilerParams(
            dimension_semantics=("arbitrary", "arbitrary"), vmem_limit_bytes=VMEM_LIMIT),
        name="in_proj_rglru",
    )(x, cos_t, sin_t, *layer_params, vbias)


def _attn_body(alpha, q_ref, k_ref, vt_ref, qn_ref, kn_ref, pa_ref, gb_ref, x_ref, wo_ref, g_ref, b_ref, o_ref,
               s_ref, m_ref, acc_ref):
    n_heads = m_ref.shape[0]
    qi = pl.program_id(1)
    key_t = _strand_time(lax.broadcasted_iota(jnp.int32, (TB, TB), 0))
    qry_t = _strand_time(lax.broadcasted_iota(jnp.int32, (TB, TB), 1))
    mask_bias = jnp.where(key_t <= qry_t, 0.0, NEG_INF)
    m_ref[...] = jnp.full(m_ref.shape, NEG_INF, F32)
    acc_ref[...] = jnp.zeros(acc_ref.shape, F32)
    head = lambda h: slice(h * HEAD_GROUP, (h + 1) * HEAD_GROUP)

    def scores(k_blk, q_blk):
        return lax.dot_general(k_blk, q_blk, (((1,), (1,)), ((), ())), preferred_element_type=F32)

    def attend(j, h, s):
        m_old = m_ref[h]
        m_new = jnp.maximum(m_old, jnp.max(s, axis=0, keepdims=True))
        p = jnp.exp2(s - m_new).astype(BF16)
        pv = _dot(vt_ref[j, h * V_ROWS:(h + 1) * V_ROWS, :], p)
        acc_ref[h] = acc_ref[h] * jnp.exp2(m_old - m_new) + pv
        m_ref[h] = m_new

    @pl.when((pl.program_id(0) == 0) & (qi == 0))
    def _():
        for h in range(n_heads):
            s_ref[h] = scores(k_ref[0:TB, head(h)], q_ref[:, head(h)])

    def body(j, carry):
        off = pl.multiple_of((j + 1) * TB, TB)
        for h in range(n_heads):
            s = s_ref[h]
            s_ref[h] = scores(k_ref[pl.ds(off, TB), head(h)], q_ref[:, head(h)])
            attend(j, h, s)
        return carry

    def run(first, count):
        for i in range(count):
            body(first + i, 0)

    def body_quad(t, carry):
        run(4 * t, 4)
        return carry

    quads = qi // 4
    rest = qi - 4 * quads
    lax.fori_loop(0, quads, body_quad, 0)

    @pl.when(rest >= 2)
    def _():
        run(4 * quads, 2)

    @pl.when(rest % 2 == 1)
    def _():
        run(qi - 1, 1)

    for h in range(n_heads):
        s = s_ref[h]
        s_ref[h] = scores(kn_ref[:, head(h)], qn_ref[:, head(h)])
        attend(qi, h, s + mask_bias)

    ys = []
    for h in range(n_heads):
        a = acc_ref[h]
        ys.append(a[:V_HEAD] / a[V_HEAD:V_HEAD + 1])
    y = jnp.concatenate(ys, axis=0).T
    merged = pa_ref[...] + (0.5 * jnp.tanh(gb_ref[...]) + 0.5) * y
    o = _dot(merged.astype(BF16), wo_ref[...])
    o_ref[...] = _layer_norm(alpha * x_ref[...] + o, g_ref[...], b_ref[...])


def _attn_out(l, q, k, vt, pa, gb, x, w_out, g, b, *, alpha):
    B, S, D = x.shape
    n_heads = k.shape[2] // HEAD_GROUP
    n_steps = S // TB
    assert vt.shape[3] == TB and vt.shape[2] == n_heads * V_ROWS

    def next_step(bi, s):
        wrap = s + 1 == n_steps
        return jnp.where(wrap, jnp.minimum(bi + 1, B - 1), bi), jnp.where(wrap, 0, s + 1)

    in_specs = [_tok_spec(TB, q.shape[2]),
                pl.BlockSpec((None, S, k.shape[2]), lambda bi, s: (bi, 0, 0)),
                pl.BlockSpec((None,) + vt.shape[1:], lambda bi, s: (bi, 0, 0, 0)),
                pl.BlockSpec((None, TB, q.shape[2]), lambda bi, s: (*next_step(bi, s), 0)),
                pl.BlockSpec((None, TB, k.shape[2]), lambda bi, s: (next_step(bi, s)[0], 0, 0)),
                _tok_spec(TB, pa.shape[2]), _tok_spec(TB, gb.shape[2]), _tok_spec(TB, D),
                _layer_spec(w_out, l), _layer_spec(g, l), _layer_spec(b, l)]
    return pl.pallas_call(
        functools.partial(_attn_body, alpha),
        grid=(B, n_steps),
        in_specs=in_specs,
        out_specs=_tok_spec(TB, D),
        out_shape=jax.ShapeDtypeStruct((B, S, D), F32),
        scratch_shapes=[pltpu.VMEM((n_heads, TB, TB), F32), pltpu.VMEM((n_heads, 1, TB), F32),
                        pltpu.VMEM((n_heads, V_ROWS, TB), F32)],
        compiler_params=pltpu.CompilerParams(
            dimension_semantics=("arbitrary", "arbitrary"), vmem_limit_bytes=VMEM_LIMIT),
        name="attn_out",
    )(q, k, vt, q, k, pa, gb, x, w_out, g, b)


def _ffn_body(alpha, x_ref, wu_ref, cw_ref, cb_ref, wd_ref, g_ref, b_ref, o_ref, hprev_ref):
    @pl.when(pl.program_id(1) == 0)
    def _():
        hprev_ref[...] = jnp.zeros_like(hprev_ref)

    d_ff = wd_ref.shape[0]
    x = x_ref[...]
    hconv = _causal_conv(hprev_ref, _dot(x.astype(BF16), wu_ref[...]), cw_ref, cb_ref, slice(None), FFN_CONV_WIDTH)
    hg, half_hv = hconv[:, :d_ff], hconv[:, d_ff:]
    act = ((hg * half_hv) * (1.0 + jnp.tanh(_gelu_tanh_arg(hg, 1.0)))).astype(BF16)
    o_ref[...] = _layer_norm(alpha * x + _dot(act, wd_ref[...]), g_ref[...], b_ref[...])


def _conv_ffn(l, x, w_up, cw, cb, w_down, g, b, *, tm, alpha):
    B, S, D = x.shape
    ins = [x, w_up, cw, cb, w_down, g, b]
    return pl.pallas_call(
        functools.partial(_ffn_body, alpha),
        grid=(B, S // tm),
        in_specs=[_tok_spec(tm, D)] + [_layer_spec(a, l) for a in ins[1:]],
        out_specs=_tok_spec(tm, D),
        out_shape=jax.ShapeDtypeStruct((B, S, D), F32),
        scratch_shapes=[pltpu.VMEM(((FFN_CONV_WIDTH - 1) * SUBLANES, w_up.shape[2]), F32)],
        compiler_params=pltpu.CompilerParams(
            dimension_semantics=("arbitrary", "arbitrary"), vmem_limit_bytes=VMEM_LIMIT),
        name="conv_ffn",
    )(*ins)


def _block_diag(w):
    L, H, r, _ = w.shape
    eye = jnp.eye(H, dtype=w.dtype)
    return (eye[:, None, :, None] * w[:, :, :, None, :]).reshape(L, H * r, H * r)


def _head_group(nope, rope):
    ref = nope if nope is not None else rope
    z = lambda n: jnp.zeros(ref.shape[:-1] + (n,), ref.dtype)
    r0, r1 = (z(ROPE_HALF), z(ROPE_HALF)) if rope is None else (rope[..., :ROPE_HALF], rope[..., ROPE_HALF:])
    split = HALF_GROUP - ROPE_HALF
    n0, n1 = (z(split), z(QK_NOPE - split)) if nope is None else (nope[..., :split], nope[..., split:])
    return jnp.concatenate([r0, n0, r1, n1, z(HEAD_GROUP - QK_NOPE - QK_ROPE)], axis=-1)


def _prepare_weights(rnn_chunk, w_in, conv_w, conv_b, gx_w, gx_b, ga_w, ga_b, lru_lambda,
                     q_norm_g, w_uq, kv_norm_g, w_ukv, w_out, ln1_g, ln1_b,
                     w_up, ffn_conv_w, ffn_conv_b, w_down, ln2_g, ln2_b):
    L = w_in.shape[0]
    d_rnn = conv_w.shape[2]
    q_lora, kv_lora = w_uq.shape[1], w_ukv.shape[1]
    d_ff = w_down.shape[1]
    p_q = 2 * d_rnn
    p_kv = p_q + q_lora
    p_kr = p_kv + kv_lora
    p_a = p_kr + QK_ROPE
    w_in_p = jnp.concatenate(
        [w_in[..., :d_rnn], 0.25 * w_in[..., d_rnn:p_q], 0.5 * w_in[..., p_a:], w_in[..., p_q:p_kr],
         _head_group(None, w_in[..., p_kr:p_a])], axis=-1).astype(BF16)
    gx_bd, ga_bd = _block_diag(0.5 * gx_w), _block_diag(0.5 * ga_w)
    ch = rnn_chunk
    wg = jnp.stack([
        jnp.concatenate([gx_bd[:, c * ch:(c + 1) * ch, c * ch:(c + 1) * ch],
                         ga_bd[:, c * ch:(c + 1) * ch, c * ch:(c + 1) * ch]], axis=-1)
        for c in range(d_rnn // ch)], axis=1).astype(BF16)
    row = lambda a: a[:, None, :]
    wq = w_uq.reshape(L, q_lora, N_HEADS, QK_NOPE + QK_ROPE)
    wq = _head_group(wq[..., :QK_NOPE], wq[..., QK_NOPE:]).reshape(L, q_lora, N_HEADS * HEAD_GROUP).astype(BF16)
    wkv = w_ukv.reshape(L, kv_lora, N_HEADS, QK_NOPE + V_HEAD)
    wk = _head_group(wkv[..., :QK_NOPE], None).reshape(L, kv_lora, N_HEADS * HEAD_GROUP).astype(BF16)
    wvt = jnp.pad(jnp.transpose(wkv[..., QK_NOPE:], (0, 2, 3, 1)), ((0, 0), (0, 0), (0, V_ROWS - V_HEAD), (0, 0)))
    wvt = wvt.reshape(L, N_HEADS * V_ROWS, kv_lora).astype(BF16)
    val_half = jnp.concatenate([jnp.ones((d_ff,), F32), jnp.full((d_ff,), 0.5, F32)])
    return dict(
        w_in_p=w_in_p, conv_w=conv_w, conv_b=row(conv_b), wg=wg, gx_b=row(0.5 * gx_b), ga_b=row(0.5 * ga_b),
        lam=row(lru_lambda), qg=row(q_norm_g), kvg=row(kv_norm_g), wq=wq, wk=wk, wvt=wvt,
        w_out=w_out.astype(BF16), ln1_g=row(ln1_g), ln1_b=row(ln1_b),
        w_up=w_up.astype(BF16), fcw=ffn_conv_w * val_half, fcb=row(ffn_conv_b * val_half),
        w_down=w_down.astype(BF16), ln2_g=row(ln2_g), ln2_b=row(ln2_b))


def kernel(x, positions, w_in, conv_w, conv_b, gx_w, gx_b, ga_w, ga_b, lru_lambda, q_norm_g, w_uq, kv_norm_g, w_ukv, w_out, ln1_g, ln1_b, w_up, ffn_conv_w, ffn_conv_b, w_down, ln2_g, ln2_b):
    depth = w_in.shape[0]
    S = x.shape[1]
    assert S % TB == 0
    alpha = (2 * depth) ** 0.25
    tm = 2 * TB if S % (2 * TB) == 0 else TB
    rnn_chunk = 256
    p = _prepare_weights(rnn_chunk, w_in, conv_w, conv_b, gx_w, gx_b, ga_w, ga_b, lru_lambda,
                         q_norm_g, w_uq, kv_norm_g, w_ukv, w_out, ln1_g, ln1_b,
                         w_up, ffn_conv_w, ffn_conv_b, w_down, ln2_g, ln2_b)
    x = _to_strands(x)
    cos_t, sin_t = _rope_tables(_to_strands(positions), S)
    ones_row = (np.arange(N_HEADS * V_ROWS) % V_ROWS == V_HEAD).astype(np.float32)
    vbias = jnp.asarray(np.repeat(ones_row[:, None], LANES, axis=1))
    for l in range(depth):
        pa, gb, q, k, vt = _in_proj_rglru(
            l, x, cos_t, sin_t, p["w_in_p"], p["conv_w"], p["conv_b"], p["wg"], p["gx_b"], p["ga_b"], p["lam"],
            p["qg"], p["kvg"], p["wq"], p["wk"], p["wvt"], vbias, tm=tm, chunk=rnn_chunk)
        x = _attn_out(l, q, k, vt, pa, gb, x, p["w_out"], p["ln1_g"], p["ln1_b"], alpha=alpha)
        x = _conv_ffn(l, x, p["w_up"], p["fcw"], p["fcb"], p["w_down"], p["ln2_g"], p["ln2_b"],
                      tm=tm, alpha=alpha)
    return _from_strands(x)
```

```python
import functools
import math

import jax
import jax.numpy as jnp
import numpy as np
from jax import lax
from jax.experimental import pallas as pl
from jax.experimental.pallas import tpu as pltpu

RNN_BLOCKS = 16
CONV_WIDTH = 4
LRU_C = 8.0
N_HEADS = 16
QK_NOPE = 64
QK_ROPE = 32
V_HEAD = 64
ROPE_THETA = 10000.0
FFN_CONV_WIDTH = 3
EPS = 1e-6
NEG_INF = -1e30

LANES = 128
SUBLANES = 8
TB = 256
ROWS = TB // SUBLANES
HEAD_GROUP = LANES
ROPE_HALF = QK_ROPE // 2
HALF_GROUP = HEAD_GROUP // 2
V_ROWS = 80
VMEM_LIMIT = 56 * 1024 * 1024
SCORE_SCALE = (QK_NOPE + QK_ROPE) ** -0.5 * math.log2(math.e)

BF16 = jnp.bfloat16
F32 = jnp.float32


def _dot(a, b):
    return jnp.dot(a, b, preferred_element_type=F32)


GELU_K0 = math.sqrt(2.0 / math.pi)
GELU_K1 = 0.044715


def _gelu_tanh_arg(x, scale):
    return x * (scale * GELU_K0 + (scale ** 3 * GELU_K0 * GELU_K1) * (x * x))


def _sqrt_nonneg(x):
    return jnp.where(x > 0.0, x * lax.rsqrt(x), 0.0)


def _layer_norm(z, g, b):
    mu = jnp.mean(z, axis=-1, keepdims=True)
    zc = z - mu
    var = jnp.mean(zc * zc, axis=-1, keepdims=True)
    return zc * lax.rsqrt(var + EPS) * g + b


def _rms_norm(z, g):
    return z * lax.rsqrt(jnp.mean(z * z, axis=-1, keepdims=True) + EPS) * g


def _to_strands(a):
    B, S = a.shape[:2]
    rest = a.shape[2:]
    return a.reshape(B, S // TB, SUBLANES, ROWS, *rest).swapaxes(2, 3).reshape(B, S, *rest)


def _from_strands(a):
    B, S = a.shape[:2]
    rest = a.shape[2:]
    return a.reshape(B, S // TB, ROWS, SUBLANES, *rest).swapaxes(2, 3).reshape(B, S, *rest)


def _strand_time(r):
    return (r & (SUBLANES - 1)) * ROWS + (r >> 3)


def _delays(prev_tail, x, depth):
    n = depth * SUBLANES
    C = x.shape[1]
    sub = lax.broadcasted_iota(jnp.int32, (depth, SUBLANES, C), 1)
    own = pltpu.roll(x[TB - n:].reshape(depth, SUBLANES, C), 1, 1)
    prev = pltpu.roll(prev_tail.reshape(depth, SUBLANES, C), 1, 1)
    head = jnp.where(sub == 0, prev, own).reshape(n, C)
    ext = jnp.concatenate([head, x], axis=0)
    return [ext[(depth - k) * SUBLANES:(depth - k) * SUBLANES + TB] for k in range(1, depth + 1)]


def _blocks(z):
    return [z[i * TB:(i + 1) * TB] for i in range(z.shape[0] // TB)]


def _causal_conv(prev_ref, z, w_ref, b_ref, cs, width):
    tail = prev_ref.shape[0]
    prev = prev_ref[:, cs]
    outs = []
    for x in _blocks(z):
        out = x * w_ref[width - 1:width, cs] + b_ref[:, cs]
        for k, xd in enumerate(_delays(prev, x, width - 1), start=1):
            out = out + xd * w_ref[width - 1 - k:width - k, cs]
        outs.append(out)
        prev = x[TB - tail:]
    prev_ref[:, cs] = prev
    return jnp.concatenate(outs, axis=0)


def _layer_spec(a, l):
    nd = a.ndim - 1
    return pl.BlockSpec((None,) + a.shape[1:], lambda *_: (l,) + (0,) * nd, pipeline_mode=pl.Buffered(1))


def _tok_spec(tm, width):
    return pl.BlockSpec((None, tm, width), lambda b, s: (b, s, 0))


def _rope_tables_body(pos_ref, invf_ref, cos_ref, sin_ref):
    pos = pos_ref[...].astype(F32)
    ang = pos * invf_ref[...]
    lane = lax.broadcasted_iota(jnp.int32, ang.shape, 1)
    cosv = jnp.cos(ang)
    sinv = jnp.sin(ang)
    rot = (lane & (HALF_GROUP - 1)) < ROPE_HALF
    cos_ref[...] = jnp.where(rot, cosv, 1.0)
    sin_ref[...] = jnp.where(rot, jnp.where(lane < HALF_GROUP, -sinv, sinv), 0.0)


def _rope_tables(positions, tm):
    B, S = positions.shape
    inv_freq = ROPE_THETA ** (-jnp.arange(0, QK_ROPE, 2, dtype=F32) / QK_ROPE)
    idx = np.minimum(np.arange(LANES) % HALF_GROUP, ROPE_HALF - 1)
    invf = inv_freq[idx].reshape(1, LANES)
    pos = positions.reshape(B, S, 1)
    out = jax.ShapeDtypeStruct((B, S, LANES), F32)
    return pl.pallas_call(
        _rope_tables_body,
        grid=(B, S // tm),
        in_specs=[_tok_spec(tm, 1), pl.BlockSpec((1, LANES), lambda b, s: (0, 0))],
        out_specs=[_tok_spec(tm, LANES), _tok_spec(tm, LANES)],
        out_shape=[out, out],
        compiler_params=pltpu.CompilerParams(dimension_semantics=("parallel", "parallel")),
        name="rope_tables",
    )(pos, invf)


def _lru_scan(a, u, h0):
    C = a.shape[1]
    row = lambda z, v: z[v * SUBLANES:(v + 1) * SUBLANES]
    p, h = row(a, 0), row(u, 0)
    ps, hs = [p], [h]
    for v in range(1, ROWS):
        av = row(a, v)
        p = av * p
        h = av * h + row(u, v)
        ps.append(p)
        hs.append(h)
    sub = lax.broadcasted_iota(jnp.int32, (SUBLANES, C), 0)
    pc, hc = p, h
    step = 1
    while step < SUBLANES:
        m = sub >= step
        rp = pltpu.roll(pc, step, 0)
        rh = pltpu.roll(hc, step, 0)
        hc = jnp.where(m, pc * rh + hc, hc)
        pc = jnp.where(m, pc * rp, pc)
        step *= 2
    end = hc + pc * h0
    init = jnp.where(sub == 0, h0, pltpu.roll(end, 1, 0))
    out = jnp.concatenate([hs[v] + ps[v] * init for v in range(ROWS)], axis=0)
    return out, end[SUBLANES - 1:]


def _in_proj_body(d_rnn, d_mix, q_lora, kv_lora, chunk,
                  x_ref, cos_ref, sin_ref, w_ref, cw_ref, cb_ref, wg_ref, gxb_ref, gab_ref, lam_ref, qg_ref, kvg_ref,
                  wq_ref, wk_ref, wvt_ref, vbias_ref,
                  pa_ref, gb_ref, q_ref, k_ref, vt_ref, xprev_ref, hc_ref):
    @pl.when(pl.program_id(1) == 0)
    def _():
        xprev_ref[...] = jnp.zeros_like(xprev_ref)
        hc_ref[...] = jnp.zeros_like(hc_ref)

    xb = x_ref[...].astype(BF16)
    neg_lam = -lam_ref[...]
    softplus = jnp.maximum(neg_lam, 0.0) + jnp.log1p(jnp.exp(-jnp.abs(neg_lam)))
    half_rate = (-0.5 * LRU_C) * softplus
    off_g, off_a, off_b = d_rnn, 2 * d_rnn, 2 * d_rnn + d_mix
    off_q = 2 * d_rnn + 2 * d_mix
    off_kv = off_q + q_lora
    off_kpe = off_kv + kv_lora

    for c in range(d_rnn // chunk):
        cs = slice(c * chunk, (c + 1) * chunk)
        xr = _dot(xb, w_ref[:, cs])
        conv = _causal_conv(xprev_ref, xr, cw_ref, cb_ref, cs, CONV_WIDTH)

        gates = _dot(conv.astype(BF16), wg_ref[c])
        tx = jnp.tanh(gates[:, :chunk] + gxb_ref[:, cs])
        tr = jnp.tanh(gates[:, chunk:] + gab_ref[:, cs])
        log_a = half_rate[:, cs] * tr + half_rate[:, cs]
        a = jnp.exp(log_a)
        t = jnp.tanh(log_a)
        half_mult = _sqrt_nonneg((-0.5 * t) / (1.0 - t))
        u = (half_mult * conv) * (tx + 1.0)
        hs, hlast = [], hc_ref[:, cs]
        for a_blk, u_blk in zip(_blocks(a), _blocks(u)):
            h_blk, hlast = _lru_scan(a_blk, u_blk, hlast)
            hs.append(h_blk)
        h = jnp.concatenate(hs, axis=0)
        hc_ref[:, cs] = hlast

        gq = _dot(xb, w_ref[:, off_g + c * chunk:off_g + (c + 1) * chunk])
        tg = jnp.tanh(_gelu_tanh_arg(gq, 4.0))
        ta = jnp.tanh(_dot(xb, w_ref[:, off_a + c * chunk:off_a + (c + 1) * chunk]))
        pa_ref[:, cs] = ((1.0 + tg) * (1.0 + ta)) * (gq * h)
        gb_ref[:, cs] = _dot(xb, w_ref[:, off_b + c * chunk:off_b + (c + 1) * chunk])

    cosv = cos_ref[...]
    sinv = sin_ref[...]

    def rope(z):
        return z * cosv + pltpu.roll(z, HALF_GROUP, 1) * sinv

    qn = _rms_norm(_dot(xb, w_ref[:, off_q:off_kv]), qg_ref[...]).astype(BF16)
    kvn = _rms_norm(_dot(xb, w_ref[:, off_kv:off_kpe]), kvg_ref[...]).astype(BF16)
    kpe = rope(_dot(xb, w_ref[:, off_kpe:off_kpe + HEAD_GROUP]))
    qf = _dot(qn, wq_ref[...])
    kf = _dot(kvn, wk_ref[...])
    for h in range(qf.shape[1] // HEAD_GROUP):
        sl = slice(h * HEAD_GROUP, (h + 1) * HEAD_GROUP)
        q_ref[:, sl] = (rope(qf[:, sl]) * SCORE_SCALE).astype(BF16)
        k_ref[:, sl] = (kf[:, sl] + kpe).astype(BF16)
    vbias = vbias_ref[...]
    for i, kvn_blk in enumerate(_blocks(kvn)):
        vt = lax.dot_general(wvt_ref[...], kvn_blk, (((1,), (1,)), ((), ())), preferred_element_type=F32)
        for c in range(TB // LANES):
            sl = slice(c * LANES, (c + 1) * LANES)
            vt_ref[i, :, sl] = (vt[:, sl] + vbias).astype(BF16)


def _in_proj_rglru(l, x, cos_t, sin_t, w_in_p, conv_w, conv_b, wg, gx_b, ga_b, lam, qg, kvg, wq, wk, wvt, vbias,
                   *, tm, chunk):
    B, S, D = x.shape
    d_rnn = conv_w.shape[2]
    q_lora, kv_lora = qg.shape[2], kvg.shape[2]
    d_mix = (w_in_p.shape[2] - 2 * d_rnn - q_lora - kv_lora - HEAD_GROUP) // 2
    body = functools.partial(_in_proj_body, d_rnn, d_mix, q_lora, kv_lora, chunk)
    layer_params = [w_in_p, conv_w, conv_b, wg, gx_b, ga_b, lam, qg, kvg, wq, wk, wvt]
    in_specs = ([_tok_spec(tm, D), _tok_spec(tm, LANES), _tok_spec(tm, LANES)]
                + [_layer_spec(a, l) for a in layer_params]
                + [pl.BlockSpec(vbias.shape, lambda b, s: (0, 0), pipeline_mode=pl.Buffered(1))])
    rows = wvt.shape[1]
    return pl.pallas_call(
        body,
        grid=(B, S // tm),
        in_specs=in_specs,
        out_specs=[_tok_spec(tm, d_mix), _tok_spec(tm, d_mix), _tok_spec(tm, wq.shape[2]), _tok_spec(tm, wk.shape[2]),
                   pl.BlockSpec((None, tm // TB, rows, TB), lambda b, s: (b, s, 0, 0))],
        out_shape=[jax.ShapeDtypeStruct((B, S, d_mix), F32), jax.ShapeDtypeStruct((B, S, d_mix), F32),
                   jax.ShapeDtypeStruct((B, S, wq.shape[2]), BF16), jax.ShapeDtypeStruct((B, S, wk.shape[2]), BF16),
                   jax.ShapeDtypeStruct((B, S // TB, rows, TB), BF16)],
        scratch_shapes=[pltpu.VMEM(((CONV_WIDTH - 1) * SUBLANES, d_rnn), F32), pltpu.VMEM((1, d_rnn), F32)],
        compiler_params=pltpu.CompilerParams(
            dimension_semantics=("arbitrary", "arbitrary"), vmem_limit_bytes=VMEM_LIMIT),
        name="in_proj_rglru",
    )(x, cos_t, sin_t, *layer_params, vbias)


def _attn_body(alpha, q_ref, k_ref, vt_ref, qn_ref, kn_ref, pa_ref, gb_ref, x_ref, wo_ref, g_ref, b_ref, o_ref,
               s_ref, m_ref, acc_ref):
    n_heads = m_ref.shape[0]
    qi = pl.program_id(1)
    key_t = _strand_time(lax.broadcasted_iota(jnp.int32, (TB, TB), 0))
    qry_t = _strand_time(lax.broadcasted_iota(jnp.int32, (TB, TB), 1))
    mask_bias = jnp.where(key_t <= qry_t, 0.0, NEG_INF)
    m_ref[...] = jnp.full(m_ref.shape, NEG_INF, F32)
    acc_ref[...] = jnp.zeros(acc_ref.shape, F32)
    head = lambda h: slice(h * HEAD_GROUP, (h + 1) * HEAD_GROUP)

    def scores(k_blk, q_blk):
        return lax.dot_general(k_blk, q_blk, (((1,), (1,)), ((), ())), preferred_element_type=F32)

    def attend(j, h, s):
        m_old = m_ref[h]
        m_new = jnp.maximum(m_old, jnp.max(s, axis=0, keepdims=True))
        p = jnp.exp2(s - m_new).astype(BF16)
        pv = _dot(vt_ref[j, h * V_ROWS:(h + 1) * V_ROWS, :], p)
        acc_ref[h] = acc_ref[h] * jnp.exp2(m_old - m_new) + pv
        m_ref[h] = m_new

    @pl.when((pl.program_id(0) == 0) & (qi == 0))
    def _():
        for h in range(n_heads):
            s_ref[h] = scores(k_ref[0:TB, head(h)], q_ref[:, head(h)])

    def body(j):
        off = pl.multiple_of((j + 1) * TB, TB)
        for h in range(n_heads):
            s = s_ref[h]
            s_ref[h] = scores(k_ref[pl.ds(off, TB), head(h)], q_ref[:, head(h)])
            attend(j, h, s)

    def diagonal():
        for h in range(n_heads):
            s = s_ref[h]
            s_ref[h] = scores(kn_ref[:, head(h)], qn_ref[:, head(h)])
            attend(qi, h, s + mask_bias)

    def body_quad(t, carry):
        for i in range(4):
            body(4 * t + i)
        return carry

    quads = qi // 4
    rest = qi - 4 * quads
    lax.fori_loop(0, quads, body_quad, 0)
    for r in range(4):
        @pl.when(rest == r)
        def _():
            for i in range(r):
                body(4 * quads + i)
            diagonal()

    ys = []
    for h in range(n_heads):
        a = acc_ref[h]
        ys.append(a[:V_HEAD] / a[V_HEAD:V_HEAD + 1])
    y = jnp.concatenate(ys, axis=0).T
    merged = pa_ref[...] + (0.5 * jnp.tanh(gb_ref[...]) + 0.5) * y
    o = _dot(merged.astype(BF16), wo_ref[...])
    o_ref[...] = _layer_norm(alpha * x_ref[...] + o, g_ref[...], b_ref[...])


def _attn_out(l, q, k, vt, pa, gb, x, w_out, g, b, *, alpha):
    B, S, D = x.shape
    n_heads = k.shape[2] // HEAD_GROUP
    n_steps = S // TB
    assert vt.shape[3] == TB and vt.shape[2] == n_heads * V_ROWS

    def next_step(bi, s):
        wrap = s + 1 == n_steps
        return jnp.where(wrap, jnp.minimum(bi + 1, B - 1), bi), jnp.where(wrap, 0, s + 1)

    in_specs = [_tok_spec(TB, q.shape[2]),
                pl.BlockSpec((None, S, k.shape[2]), lambda bi, s: (bi, 0, 0)),
                pl.BlockSpec((None,) + vt.shape[1:], lambda bi, s: (bi, 0, 0, 0)),
                pl.BlockSpec((None, TB, q.shape[2]), lambda bi, s: (*next_step(bi, s), 0)),
                pl.BlockSpec((None, TB, k.shape[2]), lambda bi, s: (next_step(bi, s)[0], 0, 0)),
                _tok_spec(TB, pa.shape[2]), _tok_spec(TB, gb.shape[2]), _tok_spec(TB, D),
                _layer_spec(w_out, l), _layer_spec(g, l), _layer_spec(b, l)]
    return pl.pallas_call(
        functools.partial(_attn_body, alpha),
        grid=(B, n_steps),
        in_specs=in_specs,
        out_specs=_tok_spec(TB, D),
        out_shape=jax.ShapeDtypeStruct((B, S, D), F32),
        scratch_shapes=[pltpu.VMEM((n_heads, TB, TB), F32), pltpu.VMEM((n_heads, 1, TB), F32),
                        pltpu.VMEM((n_heads, V_ROWS, TB), F32)],
        compiler_params=pltpu.CompilerParams(
            dimension_semantics=("arbitrary", "arbitrary"), vmem_limit_bytes=VMEM_LIMIT),
        name="attn_out",
    )(q, k, vt, q, k, pa, gb, x, w_out, g, b)


def _ffn_body(alpha, x_ref, wu_ref, cw_ref, cb_ref, wd_ref, g_ref, b_ref, o_ref, hprev_ref):
    @pl.when(pl.program_id(1) == 0)
    def _():
        hprev_ref[...] = jnp.zeros_like(hprev_ref)

    d_ff = wd_ref.shape[0]
    x = x_ref[...]
    hconv = _causal_conv(hprev_ref, _dot(x.astype(BF16), wu_ref[...]), cw_ref, cb_ref, slice(None), FFN_CONV_WIDTH)
    hg, half_hv = hconv[:, :d_ff], hconv[:, d_ff:]
    act = ((hg * half_hv) * (1.0 + jnp.tanh(_gelu_tanh_arg(hg, 1.0)))).astype(BF16)
    o_ref[...] = _layer_norm(alpha * x + _dot(act, wd_ref[...]), g_ref[...], b_ref[...])


def _conv_ffn(l, x, w_up, cw, cb, w_down, g, b, *, tm, alpha):
    B, S, D = x.shape
    ins = [x, w_up, cw, cb, w_down, g, b]
    return pl.pallas_call(
        functools.partial(_ffn_body, alpha),
        grid=(B, S // tm),
        in_specs=[_tok_spec(tm, D)] + [_layer_spec(a, l) for a in ins[1:]],
        out_specs=_tok_spec(tm, D),
        out_shape=jax.ShapeDtypeStruct((B, S, D), F32),
        scratch_shapes=[pltpu.VMEM(((FFN_CONV_WIDTH - 1) * SUBLANES, w_up.shape[2]), F32)],
        compiler_params=pltpu.CompilerParams(
            dimension_semantics=("arbitrary", "arbitrary"), vmem_limit_bytes=VMEM_LIMIT),
        name="conv_ffn",
    )(*ins)


def _block_diag(w):
    L, H, r, _ = w.shape
    eye = jnp.eye(H, dtype=w.dtype)
    return (eye[:, None, :, None] * w[:, :, :, None, :]).reshape(L, H * r, H * r)


def _head_group(nope, rope):
    ref = nope if nope is not None else rope
    z = lambda n: jnp.zeros(ref.shape[:-1] + (n,), ref.dtype)
    r0, r1 = (z(ROPE_HALF), z(ROPE_HALF)) if rope is None else (rope[..., :ROPE_HALF], rope[..., ROPE_HALF:])
    split = HALF_GROUP - ROPE_HALF
    n0, n1 = (z(split), z(QK_NOPE - split)) if nope is None else (nope[..., :split], nope[..., split:])
    return jnp.concatenate([r0, n0, r1, n1, z(HEAD_GROUP - QK_NOPE - QK_ROPE)], axis=-1)


def _prepare_weights(rnn_chunk, w_in, conv_w, conv_b, gx_w, gx_b, ga_w, ga_b, lru_lambda,
                     q_norm_g, w_uq, kv_norm_g, w_ukv, w_out, ln1_g, ln1_b,
                     w_up, ffn_conv_w, ffn_conv_b, w_down, ln2_g, ln2_b):
    L = w_in.shape[0]
    d_rnn = conv_w.shape[2]
    q_lora, kv_lora = w_uq.shape[1], w_ukv.shape[1]
    d_ff = w_down.shape[1]
    p_q = 2 * d_rnn
    p_kv = p_q + q_lora
    p_kr = p_kv + kv_lora
    p_a = p_kr + QK_ROPE
    w_in_p = jnp.concatenate(
        [w_in[..., :d_rnn], 0.25 * w_in[..., d_rnn:p_q], 0.5 * w_in[..., p_a:], w_in[..., p_q:p_kr],
         _head_group(None, w_in[..., p_kr:p_a])], axis=-1).astype(BF16)
    gx_bd, ga_bd = _block_diag(0.5 * gx_w), _block_diag(0.5 * ga_w)
    ch = rnn_chunk
    wg = jnp.stack([
        jnp.concatenate([gx_bd[:, c * ch:(c + 1) * ch, c * ch:(c + 1) * ch],
                         ga_bd[:, c * ch:(c + 1) * ch, c * ch:(c + 1) * ch]], axis=-1)
        for c in range(d_rnn // ch)], axis=1).astype(BF16)
    row = lambda a: a[:, None, :]
    wq = w_uq.reshape(L, q_lora, N_HEADS, QK_NOPE + QK_ROPE)
    wq = _head_group(wq[..., :QK_NOPE], wq[..., QK_NOPE:]).reshape(L, q_lora, N_HEADS * HEAD_GROUP).astype(BF16)
    wkv = w_ukv.reshape(L, kv_lora, N_HEADS, QK_NOPE + V_HEAD)
    wk = _head_group(wkv[..., :QK_NOPE], None).reshape(L, kv_lora, N_HEADS * HEAD_GROUP).astype(BF16)
    wvt = jnp.pad(jnp.transpose(wkv[..., QK_NOPE:], (0, 2, 3, 1)), ((0, 0), (0, 0), (0, V_ROWS - V_HEAD), (0, 0)))
    wvt = wvt.reshape(L, N_HEADS * V_ROWS, kv_lora).astype(BF16)
    val_half = jnp.concatenate([jnp.ones((d_ff,), F32), jnp.full((d_ff,), 0.5, F32)])
    return dict(
        w_in_p=w_in_p, conv_w=conv_w, conv_b=row(conv_b), wg=wg, gx_b=row(0.5 * gx_b), ga_b=row(0.5 * ga_b),
        lam=row(lru_lambda), qg=row(q_norm_g), kvg=row(kv_norm_g), wq=wq, wk=wk, wvt=wvt,
        w_out=w_out.astype(BF16), ln1_g=row(ln1_g), ln1_b=row(ln1_b),
        w_up=w_up.astype(BF16), fcw=ffn_conv_w * val_half, fcb=row(ffn_conv_b * val_half),
        w_down=w_down.astype(BF16), ln2_g=row(ln2_g), ln2_b=row(ln2_b))


def kernel(x, positions, w_in, conv_w, conv_b, gx_w, gx_b, ga_w, ga_b, lru_lambda, q_norm_g, w_uq, kv_norm_g, w_ukv, w_out, ln1_g, ln1_b, w_up, ffn_conv_w, ffn_conv_b, w_down, ln2_g, ln2_b):
    depth = w_in.shape[0]
    S = x.shape[1]
    assert S % TB == 0
    alpha = (2 * depth) ** 0.25
    tm = 2 * TB if S % (2 * TB) == 0 else TB
    rnn_chunk = 256
    p = _prepare_weights(rnn_chunk, w_in, conv_w, conv_b, gx_w, gx_b, ga_w, ga_b, lru_lambda,
                         q_norm_g, w_uq, kv_norm_g, w_ukv, w_out, ln1_g, ln1_b,
                         w_up, ffn_conv_w, ffn_conv_b, w_down, ln2_g, ln2_b)
    x = _to_strands(x)
    cos_t, sin_t = _rope_tables(_to_strands(positions), S)
    ones_row = (np.arange(N_HEADS * V_ROWS) % V_ROWS == V_HEAD).astype(np.float32)
    vbias = jnp.asarray(np.repeat(ones_row[:, None], LANES, axis=1))
    for l in range(depth):
        pa, gb, q, k, vt = _in_proj_rglru(
            l, x, cos_t, sin_t, p["w_in_p"], p["conv_w"], p["conv_b"], p["wg"], p["gx_b"], p["ga_b"], p["lam"],
            p["qg"], p["kvg"], p["wq"], p["wk"], p["wvt"], vbias, tm=tm, chunk=rnn_chunk)
        x = _attn_out(l, q, k, vt, pa, gb, x, p["w_out"], p["ln1_g"], p["ln1_b"], alpha=alpha)
        x = _conv_ffn(l, x, p["w_up"], p["fcw"], p["fcb"], p["w_down"], p["ln2_g"], p["ln2_b"],
                      tm=tm, alpha=alpha)
    return _from_strands(x)
```
